```python
import math
import jax
import jax.numpy as jnp
from jax import lax
import numpy as np

D_MODEL = 1024
BATCH = 8
SEQ = 2048
DEPTH = 1

N_SUBLAYERS = 3
ADA_COLS = 3 * N_SUBLAYERS * D_MODEL
D_FF = 2816
MLA_HEADS = 8
MLA_NOPE_DIM = 64
MLA_ROPE_DIM = 32
MLA_QK_DIM = MLA_NOPE_DIM + MLA_ROPE_DIM
MLA_V_DIM = 64
MLA_Q_LORA = 384
MLA_KV_LORA = 256
MLA_ROPE_THETA = 10000.0
MLA_WIDTH = MLA_HEADS * MLA_V_DIM
DIFF_HEADS = 4
DIFF_HEAD_DIM = 64
DIFF_V_DIM = 2 * DIFF_HEAD_DIM
DIFF_WIDTH = DIFF_HEADS * DIFF_V_DIM
ROPE_THETA = 500000.0
ROT_DIM = DIFF_HEAD_DIM // 4
N_BRANCHES = 2
IN_SIZES = (MLA_Q_LORA, MLA_KV_LORA, MLA_ROPE_DIM, DIFF_WIDTH, DIFF_WIDTH, DIFF_WIDTH, N_BRANCHES * D_MODEL)
IN_COLS = sum(IN_SIZES)
IN_SPLIT_POINTS = tuple(int(v) for v in np.cumsum(IN_SIZES)[:-1])
Q_BLOCK = 128
NORM_EPS = 1e-6

kernel_name = 'hybrid_mla_diffattn_macaron_adaln_block'


def rmsnorm(x, gain):
    xf = x.astype(jnp.float32)
    y = xf * lax.rsqrt(jnp.mean(xf * xf, axis=-1, keepdims=True) + NORM_EPS)
    return (y * gain.astype(jnp.float32)).astype(x.dtype)


def rope(x, pos, theta):
    half = x.shape[-1] // 2
    freqs = 1.0 / (theta ** (jnp.arange(half, dtype=jnp.float32) / half))
    ang = pos.astype(jnp.float32)[..., None] * freqs
    ang = ang.reshape(ang.shape[:2] + (1,) * (x.ndim - 3) + (half,))
    cos, sin = jnp.cos(ang), jnp.sin(ang)
    xf = x.astype(jnp.float32)
    x1, x2 = xf[..., :half], xf[..., half:]
    return jnp.concatenate([x1 * cos - x2 * sin, x2 * cos + x1 * sin], axis=-1).astype(x.dtype)


def partial_rope(x, pos):
    return jnp.concatenate([rope(x[..., :ROT_DIM], pos, ROPE_THETA), x[..., ROT_DIM:]], axis=-1)


def swiglu(x, w_gate, w_up, w_down):
    return (jax.nn.silu(x @ w_gate) * (x @ w_up)) @ w_down


def _to_blocks(t):
    b, s = t.shape[:2]
    return jnp.moveaxis(t.reshape((b, s // Q_BLOCK, Q_BLOCK) + t.shape[2:]), 1, 0)


def _from_blocks(t):
    nb, b, q = t.shape[:3]
    return jnp.moveaxis(t, 0, 1).reshape((b, nb * q) + t.shape[3:])


def softmax_attention(q, k, v, scale):
    def block(qb):
        s = jnp.einsum('bqhd,bkhd->bhqk', qb, k).astype(jnp.float32) * scale
        p = jax.nn.softmax(s, axis=-1).astype(v.dtype)
        return jnp.einsum('bhqk,bkhd->bqhd', p, v)
    return _from_blocks(lax.map(block, _to_blocks(q)))


def differential_attention(q1, q2, k1, k2, v, lam, scale):
    def block(qs):
        q1b, q2b = qs
        s1 = jnp.einsum('bqhd,bkhd->bhqk', q1b, k1).astype(jnp.float32) * scale
        s2 = jnp.einsum('bqhd,bkhd->bhqk', q2b, k2).astype(jnp.float32) * scale
        p = jax.nn.softmax(s1, axis=-1) - lam * jax.nn.softmax(s2, axis=-1)
        return jnp.einsum('bhqk,bkhd->bqhd', p.astype(v.dtype), v)
    return _from_blocks(lax.map(block, (_to_blocks(q1), _to_blocks(q2))))


def mla_mixer(z_q, z_kv, k_rope, pos, q_norm, w_uq, kv_norm, w_ukv, q_gain, k_gain):
    b, s, _ = z_q.shape
    q = (rmsnorm(z_q, q_norm) @ w_uq).reshape(b, s, MLA_HEADS, MLA_QK_DIM)
    kv = (rmsnorm(z_kv, kv_norm) @ w_ukv).reshape(b, s, MLA_HEADS, MLA_NOPE_DIM + MLA_V_DIM)
    k_nope, v = kv[..., :MLA_NOPE_DIM], kv[..., MLA_NOPE_DIM:]
    k_pe = jnp.broadcast_to(k_rope[:, :, None, :], (b, s, MLA_HEADS, MLA_ROPE_DIM))
    k = jnp.concatenate([k_nope, k_pe], axis=-1)
    q = rmsnorm(q, q_gain)
    k = rmsnorm(k, k_gain)
    q = jnp.concatenate([q[..., :MLA_NOPE_DIM], rope(q[..., MLA_NOPE_DIM:], pos, MLA_ROPE_THETA)], axis=-1)
    k = jnp.concatenate([k[..., :MLA_NOPE_DIM], rope(k[..., MLA_NOPE_DIM:], pos, MLA_ROPE_THETA)], axis=-1)
    o = softmax_attention(q, k, v, 1.0 / math.sqrt(MLA_QK_DIM))
    return o.reshape(b, s, MLA_WIDTH)


def diff_lambda_init(layer_idx):
    return 0.8 - 0.6 * math.exp(-0.3 * layer_idx)


def diff_mixer(z_q, z_k, z_v, pos, q_gain, k_gain, lq1, lk1, lq2, lk2, subln, lambda_init):
    b, s, _ = z_q.shape
    q = z_q.reshape(b, s, DIFF_HEADS, 2, DIFF_HEAD_DIM)
    k = z_k.reshape(b, s, DIFF_HEADS, 2, DIFF_HEAD_DIM)
    v = z_v.reshape(b, s, DIFF_HEADS, DIFF_V_DIM)
    q = partial_rope(rmsnorm(q, q_gain), pos)
    k = partial_rope(rmsnorm(k, k_gain), pos)
    f32 = jnp.float32
    lam = (jnp.exp(jnp.sum(lq1.astype(f32) * lk1.astype(f32)))
           - jnp.exp(jnp.sum(lq2.astype(f32) * lk2.astype(f32))) + lambda_init)
    o = differential_attention(q[..., 0, :], q[..., 1, :], k[..., 0, :], k[..., 1, :], v,
                               lam, 1.0 / math.sqrt(DIFF_HEAD_DIM))
    o = rmsnorm(o, subln) * (1.0 - lambda_init)
    return o.reshape(b, s, DIFF_WIDTH)


def setup_inputs(seed: int = 0) -> dict:
    key = jax.random.key(seed)
    ks = iter(jax.random.split(key, 40))
    L, D = DEPTH, D_MODEL

    def w(shape, fan_in, gain=1.0):
        return gain * fan_in ** -0.5 * jax.random.normal(next(ks), shape, jnp.float32)

    def g(shape):
        return 1.0 + 0.05 * jax.random.normal(next(ks), shape, jnp.float32)

    def small(shape, scale):
        return scale * jax.random.normal(next(ks), shape, jnp.float32)

    x = jax.random.normal(next(ks), (BATCH, SEQ, D), jnp.float32)
    c = jax.random.normal(next(ks), (BATCH, D), jnp.float32)
    positions = (jnp.arange(SEQ, dtype=jnp.int32)[None, :]
                 + jax.random.randint(next(ks), (BATCH, 1), 0, SEQ, dtype=jnp.int32))
    return {
        'x': x,
        'c': c,
        'positions': positions,
        'w_ada': w((L, D, ADA_COLS), D, 0.5),
        'b_ada': small((L, ADA_COLS), 0.02),
        'ffn1_norm': g((L, D)),
        'ffn1_w_gate': w((L, D, D_FF), D),
        'ffn1_w_up': w((L, D, D_FF), D),
        'ffn1_w_down': w((L, D_FF, D), D_FF),
        'mix_norm': g((L, D)),
        'w_in': w((L, D, IN_COLS), D),
        'mla_q_norm': g((L, MLA_Q_LORA)),
        'mla_w_uq': w((L, MLA_Q_LORA, MLA_HEADS * MLA_QK_DIM), MLA_Q_LORA),
        'mla_kv_norm': g((L, MLA_KV_LORA)),
        'mla_w_ukv': w((L, MLA_KV_LORA, MLA_HEADS * (MLA_NOPE_DIM + MLA_V_DIM)), MLA_KV_LORA),
        'mla_q_gain': g((L, MLA_QK_DIM)),
        'mla_k_gain': g((L, MLA_QK_DIM)),
        'mla_w_o': w((L, MLA_WIDTH, D), MLA_WIDTH),
        'diff_q_gain': g((L, DIFF_HEAD_DIM)),
        'diff_k_gain': g((L, DIFF_HEAD_DIM)),
        'diff_lambda_q1': small((L, DIFF_HEAD_DIM), 0.1),
        'diff_lambda_k1': small((L, DIFF_HEAD_DIM), 0.1),
        'diff_lambda_q2': small((L, DIFF_HEAD_DIM), 0.1),
        'diff_lambda_k2': small((L, DIFF_HEAD_DIM), 0.1),
        'diff_subln': g((L, DIFF_V_DIM)),
        'diff_w_o': w((L, DIFF_WIDTH, D), DIFF_WIDTH),
        'w_out': w((L, D, D), D),
        'ffn2_norm': g((L, D)),
        'ffn2_w_gate': w((L, D, D_FF), D),
        'ffn2_w_up': w((L, D, D_FF), D),
        'ffn2_w_down': w((L, D_FF, D), D_FF),
        'final_norm': g((L, D)),
    }


def reference(x, c, positions, w_ada, b_ada, ffn1_norm, ffn1_w_gate, ffn1_w_up, ffn1_w_down,
              mix_norm, w_in, mla_q_norm, mla_w_uq, mla_kv_norm, mla_w_ukv, mla_q_gain, mla_k_gain,
              mla_w_o, diff_q_gain, diff_k_gain, diff_lambda_q1, diff_lambda_k1, diff_lambda_q2,
              diff_lambda_k2, diff_subln, diff_w_o, w_out, ffn2_norm, ffn2_w_gate, ffn2_w_up,
              ffn2_w_down, final_norm):
    h = x
    cond = jax.nn.silu(c)
    for l in range(DEPTH):
        mod = cond @ w_ada[l] + b_ada[l]
        sh1, sc1, gt1, sh2, sc2, gt2, sh3, sc3, gt3 = [
            m[:, None, :] for m in jnp.split(mod, 3 * N_SUBLAYERS, axis=-1)]

        n = rmsnorm(h, ffn1_norm[l]) * (1 + sc1) + sh1
        h = h + 0.5 * gt1 * swiglu(n, ffn1_w_gate[l], ffn1_w_up[l], ffn1_w_down[l])

        n = rmsnorm(h, mix_norm[l]) * (1 + sc2) + sh2
        z = n @ w_in[l]
        zq_a, zkv_a, krope_a, zq_b, zk_b, zv_b, gate_logits = jnp.split(z, IN_SPLIT_POINTS, axis=-1)
        y_a = mla_mixer(zq_a, zkv_a, krope_a, positions, mla_q_norm[l], mla_w_uq[l],
                        mla_kv_norm[l], mla_w_ukv[l], mla_q_gain[l], mla_k_gain[l]) @ mla_w_o[l]
        y_b = diff_mixer(zq_b, zk_b, zv_b, positions, diff_q_gain[l], diff_k_gain[l],
                         diff_lambda_q1[l], diff_lambda_k1[l], diff_lambda_q2[l], diff_lambda_k2[l],
                         diff_subln[l], diff_lambda_init(l)) @ diff_w_o[l]
        gate_a, gate_b = jnp.split(jax.nn.sigmoid(gate_logits), N_BRANCHES, axis=-1)
        h = h + gt2 * ((gate_a * y_a + gate_b * y_b) @ w_out[l])

        n = rmsnorm(h, ffn2_norm[l]) * (1 + sc3) + sh3
        h = h + 0.5 * gt3 * swiglu(n, ffn2_w_gate[l], ffn2_w_up[l], ffn2_w_down[l])

        h = rmsnorm(h, final_norm[l])
    return h
```

```python
import functools
import math

import jax
import jax.numpy as jnp
from jax import lax
from jax.experimental import pallas as pl
from jax.experimental.pallas import tpu as pltpu

F32 = jnp.float32
BF16 = jnp.bfloat16

NORM_EPS = 1e-6
N_MOD = 9

MLA_HEADS = 8
MLA_NOPE = 64
MLA_ROPE = 32
MLA_QK = MLA_NOPE + MLA_ROPE
MLA_V = 64
MLA_Q_LORA = 384
MLA_KV_LORA = 256
MLA_THETA = 10000.0
DIFF_HEADS = 4
DIFF_HD = 64
DIFF_V = 2 * DIFF_HD
DIFF_THETA = 500000.0
DIFF_ROT = DIFF_HD // 4

LANES = 128
HEAD_PAD = 128
V7X_VMEM_LIMIT = 56 * 1024 * 1024

TOKEN_TILE = 512
Q_TILE = 512
FF_CHUNK = 256
ADA_COL_BLOCK = 1024


def _sigmoid(x):
    return 1.0 / (1.0 + jnp.exp(-x))


def _rms_rows(x, gain):
    ms = jnp.mean(x * x, axis=-1, keepdims=True)
    return x * lax.rsqrt(ms + NORM_EPS) * gain


def _lane_tile(g, width):
    return jnp.tile(g, (1, width // LANES))


def _const_spec(shape):
    return pl.BlockSpec(shape, lambda *_: (0,) * len(shape), pipeline_mode=pl.Buffered(1))


def _ada_kernel(c_ref, w_ref, b_ref, o_ref):
    c = c_ref[...]
    cond = c * _sigmoid(c)
    o_ref[...] = jnp.dot(cond, w_ref[...], preferred_element_type=F32,
                         precision=lax.Precision.HIGHEST) + b_ref[...]


def _ada(c, w_ada, b_ada):
    bsz, d = c.shape
    cols = w_ada.shape[1]
    return pl.pallas_call(
        _ada_kernel,
        grid=(cols // ADA_COL_BLOCK,),
        in_specs=[
            pl.BlockSpec((bsz, d), lambda j: (0, 0)),
            pl.BlockSpec((d, ADA_COL_BLOCK), lambda j: (0, j)),
            pl.BlockSpec((1, ADA_COL_BLOCK), lambda j: (0, j)),
        ],
        out_specs=pl.BlockSpec((bsz, ADA_COL_BLOCK), lambda j: (0, j)),
        out_shape=jax.ShapeDtypeStruct((bsz, cols), F32),
        compiler_params=pltpu.CompilerParams(dimension_semantics=("arbitrary",)),
        name="ada",
    )(c, w_ada, b_ada.reshape(1, cols))


def _ffn_kernel(*refs, sub, final):
    if final:
        x_ref, mod_ref, gain_ref, wg_ref, wu_ref, wd_ref, fgain_ref, o_ref, a_ref = refs
    else:
        x_ref, mod_ref, gain_ref, wg_ref, wu_ref, wd_ref, o_ref, a_ref = refs
    x = x_ref[0]
    shift = mod_ref[0, 3 * sub:3 * sub + 1, :]
    scale = mod_ref[0, 3 * sub + 1:3 * sub + 2, :]
    gate = mod_ref[0, 3 * sub + 2:3 * sub + 3, :]
    n = (_rms_rows(x, gain_ref[...]) * (1.0 + scale) + shift).astype(BF16)
    d_ff = wg_ref.shape[1]
    for c0 in range(0, d_ff, FF_CHUNK):
        g = jnp.dot(n, wg_ref[:, c0:c0 + FF_CHUNK], preferred_element_type=F32)
        u = jnp.dot(n, wu_ref[:, c0:c0 + FF_CHUNK], preferred_element_type=F32)
        a_ref[:, c0:c0 + FF_CHUNK] = ((g * _sigmoid(g)) * u).astype(BF16)
    f = jnp.dot(a_ref[...], wd_ref[...], preferred_element_type=F32)
    h = x + (0.5 * gate) * f
    if final:
        h = _rms_rows(h, fgain_ref[...])
    o_ref[0] = h


def _ffn(h, mod, gain, wg, wu, wd, *, sub, final_gain=None):
    bsz, s, d = h.shape
    d_ff = wg.shape[1]
    tm = TOKEN_TILE
    final = final_gain is not None
    in_specs = [
        pl.BlockSpec((1, tm, d), lambda b, i: (b, i, 0)),
        pl.BlockSpec((1, N_MOD, d), lambda b, i: (b, 0, 0)),
        _const_spec((1, d)),
        _const_spec((d, d_ff)),
        _const_spec((d, d_ff)),
        _const_spec((d_ff, d)),
    ]
    args = [h, mod, gain.reshape(1, d), wg, wu, wd]
    if final:
        in_specs.append(_const_spec((1, d)))
        args.append(final_gain.reshape(1, d))
    return pl.pallas_call(
        functools.partial(_ffn_kernel, sub=sub, final=final),
        grid=(bsz, s // tm),
        in_specs=in_specs,
        out_specs=pl.BlockSpec((1, tm, d), lambda b, i: (b, i, 0)),
        out_shape=jax.ShapeDtypeStruct((bsz, s, d), F32),
        scratch_shapes=[pltpu.VMEM((tm, d_ff), BF16)],
        compiler_params=pltpu.CompilerParams(
            dimension_semantics=("arbitrary", "arbitrary"),
            vmem_limit_bytes=V7X_VMEM_LIMIT),
        name="ffn%d" % sub,
    )(*args)


def _rope_rows(x1, x2, cos, sin):
    return x1 * cos - x2 * sin, x2 * cos + x1 * sin


def _prep_kernel(h_ref, mod_ref, gain_ref, pos_ref, win_ref, wuq_ref, wukv_ref,
                 qnorm_ref, kvnorm_ref, qgain_a_ref, kgain_a_ref, qgain_b_ref, kgain_b_ref,
                 freq_a_ref, freq_b_ref,
                 qa_ref, ka_ref, va_ref, qb_ref, kb_ref, vb_ref, g_ref):
    tm = h_ref.shape[1]
    x = h_ref[0]
    shift = mod_ref[0, 3:4, :]
    scale = mod_ref[0, 4:5, :]
    n = (_rms_rows(x, gain_ref[...]) * (1.0 + scale) + shift).astype(BF16)

    def proj_t(r0, r1):
        return lax.dot_general(win_ref[r0:r1, :], n, (((1,), (1,)), ((), ())),
                               preferred_element_type=F32)

    pos = pos_ref[0].astype(F32)
    ang_a = pos * _lane_tile(freq_a_ref[...], tm)
    cos_a, sin_a = jnp.cos(ang_a), jnp.sin(ang_a)
    ang_b = pos * _lane_tile(freq_b_ref[...], tm)
    cos_b, sin_b = jnp.cos(ang_b), jnp.sin(ang_b)

    o_q, o_kv, o_kr = 0, MLA_Q_LORA, MLA_Q_LORA + MLA_KV_LORA
    o_qb = o_kr + MLA_ROPE
    w_b = DIFF_HEADS * DIFF_V
    o_kb, o_vb, o_g = o_qb + w_b, o_qb + 2 * w_b, o_qb + 3 * w_b
    n_gate = g_ref.shape[1]

    z_a = proj_t(o_q, o_qb)
    zq, zkv, kr = z_a[o_q:o_kv], z_a[o_kv:o_kr], z_a[o_kr:o_qb]
    rq = lax.rsqrt(jnp.mean(zq * zq, axis=0, keepdims=True) + NORM_EPS)
    zqn = (zq * rq * _lane_tile(qnorm_ref[...], tm)).astype(BF16)
    q_all = jnp.dot(wuq_ref[...], zqn, preferred_element_type=F32)
    rkv = lax.rsqrt(jnp.mean(zkv * zkv, axis=0, keepdims=True) + NORM_EPS)
    zkvn = (zkv * rkv * _lane_tile(kvnorm_ref[...], tm)).astype(BF16)
    kv_all = jnp.dot(wukv_ref[...], zkvn, preferred_element_type=F32)

    qgain_a = _lane_tile(qgain_a_ref[...], tm)
    kgain_a = _lane_tile(kgain_a_ref[...], tm)
    q_scale = 1.0 / math.sqrt(MLA_QK)
    pad_rows = jnp.zeros((HEAD_PAD - MLA_QK, tm), F32)
    half = MLA_ROPE // 2

    def head_norm_rope_a(xh, gain):
        r = lax.rsqrt(jnp.sum(xh * xh, axis=0, keepdims=True) * (1.0 / MLA_QK) + NORM_EPS)
        xh = xh * r * gain
        r1, r2 = _rope_rows(xh[MLA_NOPE:MLA_NOPE + half], xh[MLA_NOPE + half:MLA_QK], cos_a, sin_a)
        return jnp.concatenate([xh[:MLA_NOPE], r1, r2, xh[MLA_QK:]], axis=0)

    for hd in range(MLA_HEADS):
        lo = hd * HEAD_PAD
        qh = head_norm_rope_a(q_all[lo:lo + HEAD_PAD], qgain_a) * q_scale
        qa_ref[0, lo:lo + HEAD_PAD, :] = qh.astype(BF16)
        kvh = kv_all[lo:lo + HEAD_PAD]
        va_ref[0, hd * MLA_V:(hd + 1) * MLA_V, :] = kvh[MLA_NOPE:].astype(BF16)
        kh = jnp.concatenate([kvh[:MLA_NOPE], kr, pad_rows], axis=0)
        kh = head_norm_rope_a(kh, kgain_a)
        ka_ref[0, :, lo:lo + HEAD_PAD] = kh.T.astype(BF16)

    qgain_b = _lane_tile(qgain_b_ref[...], tm)
    kgain_b = _lane_tile(kgain_b_ref[...], tm)
    qb_scale = 1.0 / math.sqrt(DIFF_HD)
    hb = DIFF_ROT // 2

    def head_norm_rope_b(xh, gain):
        r = lax.rsqrt(jnp.mean(xh * xh, axis=0, keepdims=True) + NORM_EPS)
        xh = xh * r * gain
        r1, r2 = _rope_rows(xh[:hb], xh[hb:DIFF_ROT], cos_b, sin_b)
        return jnp.concatenate([r1, r2, xh[DIFF_ROT:]], axis=0)

    zqb = proj_t(o_qb, o_kb)
    zkb = proj_t(o_kb, o_vb)
    k_parts = []
    for blk in range(2 * DIFF_HEADS):
        lo = blk * DIFF_HD
        qh = head_norm_rope_b(zqb[lo:lo + DIFF_HD], qgain_b) * qb_scale
        qb_ref[0, lo:lo + DIFF_HD, :] = qh.astype(BF16)
        k_parts.append(head_norm_rope_b(zkb[lo:lo + DIFF_HD], kgain_b))
    for hd in range(DIFF_HEADS):
        k12 = jnp.concatenate(k_parts[2 * hd:2 * hd + 2], axis=0)
        kb_ref[0, :, hd * DIFF_V:(hd + 1) * DIFF_V] = k12.T.astype(BF16)
    vb_ref[0] = proj_t(o_vb, o_g).astype(BF16)

    g_chunk = 512
    for r0 in range(0, n_gate, g_chunk):
        g_ref[0, r0:r0 + g_chunk, :] = _sigmoid(proj_t(o_g + r0, o_g + r0 + g_chunk)).astype(BF16)


def _prep(h, mod, gain, pos3, w):
    bsz, s, d = h.shape
    tm = TOKEN_TILE
    n_in = w["win_t"].shape[0]
    wa, wb = MLA_HEADS * HEAD_PAD, DIFF_HEADS * DIFF_V
    n_gate = n_in - (MLA_Q_LORA + MLA_KV_LORA + MLA_ROPE + 3 * wb)

    def fm(rows):
        return (jax.ShapeDtypeStruct((bsz, rows, s), BF16),
                pl.BlockSpec((1, rows, tm), lambda b, i: (b, 0, i)))

    def tmaj(cols):
        return (jax.ShapeDtypeStruct((bsz, s, cols), BF16),
                pl.BlockSpec((1, tm, cols), lambda b, i: (b, i, 0)))

    outs = [fm(wa), tmaj(wa), fm(MLA_HEADS * MLA_V), fm(wb), tmaj(wb), fm(wb), fm(n_gate)]
    consts = [w["win_t"], w["wuq_t"], w["wukv_t"], w["qnorm"], w["kvnorm"],
              w["qgain_a"], w["kgain_a"], w["qgain_b"], w["kgain_b"], w["freq_a"], w["freq_b"]]
    in_specs = [
        pl.BlockSpec((1, tm, d), lambda b, i: (b, i, 0)),
        pl.BlockSpec((1, N_MOD, d), lambda b, i: (b, 0, 0)),
        _const_spec((1, d)),
        pl.BlockSpec((1, 1, tm), lambda b, i: (b, 0, i)),
    ] + [_const_spec(a.shape) for a in consts]
    return pl.pallas_call(
        _prep_kernel,
        grid=(bsz, s // tm),
        in_specs=in_specs,
        out_specs=[o[1] for o in outs],
        out_shape=[o[0] for o in outs],
        compiler_params=pltpu.CompilerParams(
            dimension_semantics=("arbitrary", "arbitrary"),
            vmem_limit_bytes=V7X_VMEM_LIMIT),
        name="prep",
    )(h, mod, gain.reshape(1, d), pos3, *consts)


def _softmax_cols(s):
    m = jnp.max(s, axis=0, keepdims=True)
    p = jnp.exp(s - m)
    return p, jnp.sum(p, axis=0, keepdims=True)


def _mla_kernel(k_ref, q_ref, v_ref, o_ref):
    k = k_ref[0]
    v = v_ref[0]
    s_len = k.shape[0]
    for q0 in range(0, s_len, Q_TILE):
        q = q_ref[0, :, q0:q0 + Q_TILE]
        sc = jnp.dot(k, q, preferred_element_type=F32)
        p, l = _softmax_cols(sc)
        o = jnp.dot(v, p.astype(BF16), preferred_element_type=F32)
        o_ref[0, :, q0:q0 + Q_TILE] = (o * (1.0 / l)).astype(BF16)


def _mla_attention(ka, qa, va):
    bsz, s, _ = ka.shape
    return pl.pallas_call(
        _mla_kernel,
        grid=(bsz, MLA_HEADS),
        in_specs=[
            pl.BlockSpec((1, s, HEAD_PAD), lambda b, h: (b, 0, h)),
            pl.BlockSpec((1, HEAD_PAD, s), lambda b, h: (b, h, 0)),
            pl.BlockSpec((1, MLA_V, s), lambda b, h: (b, h, 0)),
        ],
        out_specs=pl.BlockSpec((1, MLA_V, s), lambda b, h: (b, h, 0)),
        out_shape=jax.ShapeDtypeStruct((bsz, MLA_HEADS * MLA_V, s), BF16),
        compiler_params=pltpu.CompilerParams(
            dimension_semantics=("arbitrary", "arbitrary"),
            vmem_limit_bytes=V7X_VMEM_LIMIT),
        name="mla_attn",
    )(ka, qa, va)


def _diff_kernel(k_ref, q_ref, v_ref, lq1_ref, lk1_ref, lq2_ref, lk2_ref, subln_ref, o_ref, *,
                 lambda_init):
    k = k_ref[0]
    v = v_ref[0]
    s_len = k.shape[0]
    lam = (jnp.exp(jnp.sum(lq1_ref[...] * lk1_ref[...], axis=-1, keepdims=True))
           - jnp.exp(jnp.sum(lq2_ref[...] * lk2_ref[...], axis=-1, keepdims=True))
           + lambda_init)
    subln = _lane_tile(subln_ref[...], Q_TILE)
    zeros = jnp.zeros((DIFF_HD, Q_TILE), BF16)
    for q0 in range(0, s_len, Q_TILE):
        q12 = q_ref[0, :, q0:q0 + Q_TILE]
        q1 = jnp.concatenate([q12[:DIFF_HD], zeros], axis=0)
        q2 = jnp.concatenate([zeros, q12[DIFF_HD:]], axis=0)
        p1, l1 = _softmax_cols(jnp.dot(k, q1, preferred_element_type=F32))
        p2, l2 = _softmax_cols(jnp.dot(k, q2, preferred_element_type=F32))
        d = (p1 - (lam * l1 / l2) * p2).astype(BF16)
        o = jnp.dot(v, d, preferred_element_type=F32) * (1.0 / l1)
        r = lax.rsqrt(jnp.mean(o * o, axis=0, keepdims=True) + NORM_EPS)
        o_ref[0, :, q0:q0 + Q_TILE] = ((o * r * subln) * (1.0 - lambda_init)).astype(BF16)


def _diff_attention(kb, qb, vb, lq1, lk1, lq2, lk2, subln, lambda_init):
    bsz, s, _ = kb.shape
    vec = _const_spec((1, DIFF_HD))
    return pl.pallas_call(
        functools.partial(_diff_kernel, lambda_init=lambda_init),
        grid=(bsz, DIFF_HEADS),
        in_specs=[
            pl.BlockSpec((1, s, DIFF_V), lambda b, h: (b, 0, h)),
            pl.BlockSpec((1, DIFF_V, s), lambda b, h: (b, h, 0)),
            pl.BlockSpec((1, DIFF_V, s), lambda b, h: (b, h, 0)),
            vec, vec, vec, vec,
            _const_spec((DIFF_V, LANES)),
        ],
        out_specs=pl.BlockSpec((1, DIFF_V, s), lambda b, h: (b, h, 0)),
        out_shape=jax.ShapeDtypeStruct((bsz, DIFF_HEADS * DIFF_V, s), BF16),
        compiler_params=pltpu.CompilerParams(
            dimension_semantics=("arbitrary", "arbitrary"),
            vmem_limit_bytes=V7X_VMEM_LIMIT),
        name="diff_attn",
    )(kb, qb, vb, lq1, lk1, lq2, lk2, subln)


def _merge_kernel(oa_ref, ob_ref, g_ref, h_ref, mod_ref, woa_ref, wob_ref, wout_ref, o_ref):
    d = h_ref.shape[2]
    ya = jnp.dot(woa_ref[...], oa_ref[0], preferred_element_type=F32)
    yb = jnp.dot(wob_ref[...], ob_ref[0], preferred_element_type=F32)
    ga = g_ref[0, :d, :].astype(F32)
    gb = g_ref[0, d:, :].astype(F32)
    mix = (ga * ya + gb * yb).astype(BF16)
    y_t = jnp.dot(wout_ref[...], mix, preferred_element_type=F32)
    gate = mod_ref[0, 5:6, :]
    o_ref[0] = h_ref[0] + gate * y_t.T


def _merge(oa, ob, g, h, mod, woa_t, wob_t, wout_t):
    bsz, s, d = h.shape
    tm = TOKEN_TILE
    return pl.pallas_call(
        _merge_kernel,
        grid=(bsz, s // tm),
        in_specs=[
            pl.BlockSpec((1, oa.shape[1], tm), lambda b, i: (b, 0, i)),
            pl.BlockSpec((1, ob.shape[1], tm), lambda b, i: (b, 0, i)),
            pl.BlockSpec((1, g.shape[1], tm), lambda b, i: (b, 0, i)),
            pl.BlockSpec((1, tm, d), lambda b, i: (b, i, 0)),
            pl.BlockSpec((1, N_MOD, d), lambda b, i: (b, 0, 0)),
            _const_spec(woa_t.shape),
            _const_spec(wob_t.shape),
            _const_spec(wout_t.shape),
        ],
        out_specs=pl.BlockSpec((1, tm, d), lambda b, i: (b, i, 0)),
        out_shape=jax.ShapeDtypeStruct((bsz, s, d), F32),
        compiler_params=pltpu.CompilerParams(
            dimension_semantics=("arbitrary", "arbitrary"),
            vmem_limit_bytes=V7X_VMEM_LIMIT),
        name="merge",
    )(oa, ob, g, h, mod, woa_t, wob_t, wout_t)


def _lane_bcast(v, rows=None):
    n = v.shape[0]
    out = jnp.broadcast_to(v.astype(F32)[:, None], (n, LANES))
    if rows is not None and rows > n:
        out = jnp.pad(out, ((0, rows - n), (0, 0)))
    return out


def _rope_freqs(half, theta):
    return 1.0 / (theta ** (jnp.arange(half, dtype=F32) / half))


def _layer_weights(l, w_in, mla_q_norm, mla_w_uq, mla_kv_norm, mla_w_ukv, mla_q_gain, mla_k_gain,
                   diff_q_gain, diff_k_gain):
    wuq = mla_w_uq[l].reshape(MLA_Q_LORA, MLA_HEADS, MLA_QK)
    wuq = jnp.pad(wuq, ((0, 0), (0, 0), (0, HEAD_PAD - MLA_QK))).reshape(MLA_Q_LORA, -1)
    return {
        "win_t": w_in[l].T.astype(BF16),
        "wuq_t": wuq.T.astype(BF16),
        "wukv_t": mla_w_ukv[l].T.astype(BF16),
        "qnorm": _lane_bcast(mla_q_norm[l]),
        "kvnorm": _lane_bcast(mla_kv_norm[l]),
        "qgain_a": _lane_bcast(mla_q_gain[l], HEAD_PAD),
        "kgain_a": _lane_bcast(mla_k_gain[l], HEAD_PAD),
        "qgain_b": _lane_bcast(diff_q_gain[l]),
        "kgain_b": _lane_bcast(diff_k_gain[l]),
        "freq_a": _lane_bcast(_rope_freqs(MLA_ROPE // 2, MLA_THETA)),
        "freq_b": _lane_bcast(_rope_freqs(DIFF_ROT // 2, DIFF_THETA)),
    }


def kernel(x, c, positions, w_ada, b_ada, ffn1_norm, ffn1_w_gate, ffn1_w_up, ffn1_w_down, mix_norm, w_in, mla_q_norm, mla_w_uq, mla_kv_norm, mla_w_ukv, mla_q_gain, mla_k_gain, mla_w_o, diff_q_gain, diff_k_gain, diff_lambda_q1, diff_lambda_k1, diff_lambda_q2, diff_lambda_k2, diff_subln, diff_w_o, w_out, ffn2_norm, ffn2_w_gate, ffn2_w_up, ffn2_w_down, final_norm):
    bsz, s, d = x.shape
    depth = w_ada.shape[0]
    pos3 = positions.reshape(bsz, 1, s)
    h = x
    for l in range(depth):
        lambda_init = 0.8 - 0.6 * math.exp(-0.3 * l)
        mod = _ada(c, w_ada[l], b_ada[l]).reshape(bsz, N_MOD, d)

        h = _ffn(h, mod, ffn1_norm[l], ffn1_w_gate[l].astype(BF16), ffn1_w_up[l].astype(BF16),
                 ffn1_w_down[l].astype(BF16), sub=0)

        w = _layer_weights(l, w_in, mla_q_norm, mla_w_uq, mla_kv_norm, mla_w_ukv, mla_q_gain,
                           mla_k_gain, diff_q_gain, diff_k_gain)
        qa, ka, va, qb, kb, vb, g = _prep(h, mod, mix_norm[l], pos3, w)
        oa = _mla_attention(ka, qa, va)
        ob = _diff_attention(kb, qb, vb,
                             diff_lambda_q1[l].reshape(1, -1), diff_lambda_k1[l].reshape(1, -1),
                             diff_lambda_q2[l].reshape(1, -1), diff_lambda_k2[l].reshape(1, -1),
                             _lane_bcast(diff_subln[l]), lambda_init)
        h = _merge(oa, ob, g, h, mod, mla_w_o[l].T.astype(BF16), diff_w_o[l].T.astype(BF16),
                   w_out[l].T.astype(BF16))

        h = _ffn(h, mod, ffn2_norm[l], ffn2_w_gate[l].astype(BF16), ffn2_w_up[l].astype(BF16),
                 ffn2_w_down[l].astype(BF16), sub=2, final_gain=final_norm[l])
    return h
```

```python
import functools
import math

import jax
import jax.numpy as jnp
from jax import lax
from jax.experimental import pallas as pl
from jax.experimental.pallas import tpu as pltpu

F32 = jnp.float32
BF16 = jnp.bfloat16

NORM_EPS = 1e-6
N_MOD = 9

MLA_HEADS = 8
MLA_NOPE = 64
MLA_ROPE = 32
MLA_QK = MLA_NOPE + MLA_ROPE
MLA_V = 64
MLA_Q_LORA = 384
MLA_KV_LORA = 256
MLA_THETA = 10000.0
DIFF_HEADS = 4
DIFF_HD = 64
DIFF_V = 2 * DIFF_HD
DIFF_THETA = 500000.0
DIFF_ROT = DIFF_HD // 4

LANES = 128
HEAD_PAD = 128
SUM_ROWS = 16
MLA_VS = MLA_V + SUM_ROWS
DIFF_VS = DIFF_V + SUM_ROWS
LOG2E = math.log2(math.e)
V7X_VMEM_LIMIT = 56 * 1024 * 1024

TOKEN_TILE = 512
Q_TILE = 512
KEY_CHUNK = 512
FF_CHUNK = 256
ADA_COL_BLOCK = 1024


def _sigmoid(x):
    return 1.0 / (1.0 + jnp.exp(-x))


def _rms_rows(x, gain):
    ms = jnp.mean(x * x, axis=-1, keepdims=True)
    return x * lax.rsqrt(ms + NORM_EPS) * gain


def _lane_tile(g, width):
    return jnp.tile(g, (1, width // LANES))


def _const_spec(shape):
    return pl.BlockSpec(shape, lambda *_: (0,) * len(shape), pipeline_mode=pl.Buffered(1))


def _ada_kernel(c_ref, w_ref, b_ref, o_ref):
    c = c_ref[...]
    cond = c * _sigmoid(c)
    o_ref[...] = jnp.dot(cond, w_ref[...], preferred_element_type=F32,
                         precision=lax.Precision.HIGHEST) + b_ref[...]


def _ada(c, w_ada, b_ada):
    bsz, d = c.shape
    cols = w_ada.shape[1]
    return pl.pallas_call(
        _ada_kernel,
        grid=(cols // ADA_COL_BLOCK,),
        in_specs=[
            pl.BlockSpec((bsz, d), lambda j: (0, 0)),
            pl.BlockSpec((d, ADA_COL_BLOCK), lambda j: (0, j)),
            pl.BlockSpec((1, ADA_COL_BLOCK), lambda j: (0, j)),
        ],
        out_specs=pl.BlockSpec((bsz, ADA_COL_BLOCK), lambda j: (0, j)),
        out_shape=jax.ShapeDtypeStruct((bsz, cols), F32),
        compiler_params=pltpu.CompilerParams(dimension_semantics=("arbitrary",)),
        name="ada",
    )(c, w_ada, b_ada.reshape(1, cols))


def _ffn_kernel(*refs, sub, final):
    if final:
        x_ref, mod_ref, gain_ref, wg_ref, wu_ref, wd_ref, fgain_ref, o_ref, a_ref = refs
    else:
        x_ref, mod_ref, gain_ref, wg_ref, wu_ref, wd_ref, o_ref, a_ref = refs
    x = x_ref[0]
    shift = mod_ref[0, 3 * sub:3 * sub + 1, :]
    scale = mod_ref[0, 3 * sub + 1:3 * sub + 2, :]
    gate = mod_ref[0, 3 * sub + 2:3 * sub + 3, :]
    n = (_rms_rows(x, gain_ref[...]) * (1.0 + scale) + shift).astype(BF16)
    d_ff = wg_ref.shape[1]
    for c0 in range(0, d_ff, FF_CHUNK):
        g = jnp.dot(n, wg_ref[:, c0:c0 + FF_CHUNK], preferred_element_type=F32)
        u = jnp.dot(n, wu_ref[:, c0:c0 + FF_CHUNK], preferred_element_type=F32)
        a_ref[:, c0:c0 + FF_CHUNK] = ((g * _sigmoid(g)) * u).astype(BF16)
    f = jnp.dot(a_ref[...], wd_ref[...], preferred_element_type=F32)
    h = x + (0.5 * gate) * f
    if final:
        h = _rms_rows(h, fgain_ref[...])
    o_ref[0] = h


def _ffn(h, mod, gain, wg, wu, wd, *, sub, final_gain=None):
    bsz, s, d = h.shape
    d_ff = wg.shape[1]
    tm = TOKEN_TILE
    final = final_gain is not None
    in_specs = [
        pl.BlockSpec((1, tm, d), lambda b, i: (b, i, 0)),
        pl.BlockSpec((1, N_MOD, d), lambda b, i: (b, 0, 0)),
        _const_spec((1, d)),
        _const_spec((d, d_ff)),
        _const_spec((d, d_ff)),
        _const_spec((d_ff, d)),
    ]
    args = [h, mod, gain.reshape(1, d), wg, wu, wd]
    if final:
        in_specs.append(_const_spec((1, d)))
        args.append(final_gain.reshape(1, d))
    return pl.pallas_call(
        functools.partial(_ffn_kernel, sub=sub, final=final),
        grid=(bsz, s // tm),
        in_specs=in_specs,
        out_specs=pl.BlockSpec((1, tm, d), lambda b, i: (b, i, 0)),
        out_shape=jax.ShapeDtypeStruct((bsz, s, d), F32),
        scratch_shapes=[pltpu.VMEM((tm, d_ff), BF16)],
        compiler_params=pltpu.CompilerParams(
            dimension_semantics=("arbitrary", "arbitrary"),
            vmem_limit_bytes=V7X_VMEM_LIMIT),
        name="ffn%d" % sub,
    )(*args)


def _rope_rows(x1, x2, cos, sin):
    return x1 * cos - x2 * sin, x2 * cos + x1 * sin


def _prep_kernel(h_ref, mod_ref, gain_ref, pos_ref, win_ref, wuq_ref, wukv_ref,
                 qnorm_ref, kvnorm_ref, qgain_a_ref, kgain_a_ref, qgain_b_ref, kgain_b_ref,
                 freq_a_ref, freq_b_ref,
                 qa_ref, ka_ref, va_ref, qb_ref, kb_ref, vb_ref, g_ref):
    tm = h_ref.shape[1]
    x = h_ref[0]
    shift = mod_ref[0, 3:4, :]
    scale = mod_ref[0, 4:5, :]
    n = (_rms_rows(x, gain_ref[...]) * (1.0 + scale) + shift).astype(BF16)

    def proj_t(r0, r1):
        return lax.dot_general(win_ref[r0:r1, :], n, (((1,), (1,)), ((), ())),
                               preferred_element_type=F32)

    pos = pos_ref[0].astype(F32)
    ang_a = pos * _lane_tile(freq_a_ref[...], tm)
    cos_a, sin_a = jnp.cos(ang_a), jnp.sin(ang_a)
    ang_b = pos * _lane_tile(freq_b_ref[...], tm)
    cos_b, sin_b = jnp.cos(ang_b), jnp.sin(ang_b)

    o_q, o_kv, o_kr = 0, MLA_Q_LORA, MLA_Q_LORA + MLA_KV_LORA
    o_qb = o_kr + MLA_ROPE
    w_b = DIFF_HEADS * DIFF_V
    o_kb, o_vb, o_g = o_qb + w_b, o_qb + 2 * w_b, o_qb + 3 * w_b
    n_gate = g_ref.shape[1]

    z_a = proj_t(o_q, o_qb)
    zq, zkv, kr = z_a[o_q:o_kv], z_a[o_kv:o_kr], z_a[o_kr:o_qb]
    rq = lax.rsqrt(jnp.mean(zq * zq, axis=0, keepdims=True) + NORM_EPS)
    zqn = (zq * rq * _lane_tile(qnorm_ref[...], tm)).astype(BF16)
    q_all = jnp.dot(wuq_ref[...], zqn, preferred_element_type=F32)
    rkv = lax.rsqrt(jnp.mean(zkv * zkv, axis=0, keepdims=True) + NORM_EPS)
    zkvn = (zkv * rkv * _lane_tile(kvnorm_ref[...], tm)).astype(BF16)
    kv_all = jnp.dot(wukv_ref[...], zkvn, preferred_element_type=F32)

    qgain_a = _lane_tile(qgain_a_ref[...], tm)
    kgain_a = _lane_tile(kgain_a_ref[...], tm)
    q_scale = LOG2E / math.sqrt(MLA_QK)
    pad_rows = jnp.zeros((HEAD_PAD - MLA_QK, tm), F32)
    sum_rows = (lax.broadcasted_iota(jnp.int32, (SUM_ROWS, tm), 0) == 0).astype(BF16)
    half = MLA_ROPE // 2

    def head_norm_rope_a(xh, gain):
        r = lax.rsqrt(jnp.sum(xh * xh, axis=0, keepdims=True) * (1.0 / MLA_QK) + NORM_EPS)
        xh = xh * r * gain
        r1, r2 = _rope_rows(xh[MLA_NOPE:MLA_NOPE + half], xh[MLA_NOPE + half:MLA_QK], cos_a, sin_a)
        return jnp.concatenate([xh[:MLA_NOPE], r1, r2, xh[MLA_QK:]], axis=0)

    for hd in range(MLA_HEADS):
        lo = hd * HEAD_PAD
        qh = head_norm_rope_a(q_all[lo:lo + HEAD_PAD], qgain_a) * q_scale
        qa_ref[0, lo:lo + HEAD_PAD, :] = qh.astype(BF16)
        kvh = kv_all[lo:lo + HEAD_PAD]
        va_ref[0, hd * MLA_VS:hd * MLA_VS + MLA_V, :] = kvh[MLA_NOPE:].astype(BF16)
        va_ref[0, hd * MLA_VS + MLA_V:(hd + 1) * MLA_VS, :] = sum_rows
        kh = jnp.concatenate([kvh[:MLA_NOPE], kr, pad_rows], axis=0)
        kh = head_norm_rope_a(kh, kgain_a)
        ka_ref[0, :, lo:lo + HEAD_PAD] = kh.T.astype(BF16)

    qgain_b = _lane_tile(qgain_b_ref[...], tm)
    kgain_b = _lane_tile(kgain_b_ref[...], tm)
    qb_scale = LOG2E / math.sqrt(DIFF_HD)
    hb = DIFF_ROT // 2

    def head_norm_rope_b(xh, gain):
        r = lax.rsqrt(jnp.mean(xh * xh, axis=0, keepdims=True) + NORM_EPS)
        xh = xh * r * gain
        r1, r2 = _rope_rows(xh[:hb], xh[hb:DIFF_ROT], cos_b, sin_b)
        return jnp.concatenate([r1, r2, xh[DIFF_ROT:]], axis=0)

    zqb = proj_t(o_qb, o_kb)
    zkb = proj_t(o_kb, o_vb)
    k_parts = []
    for blk in range(2 * DIFF_HEADS):
        lo = blk * DIFF_HD
        qh = head_norm_rope_b(zqb[lo:lo + DIFF_HD], qgain_b) * qb_scale
        qb_ref[0, lo:lo + DIFF_HD, :] = qh.astype(BF16)
        k_parts.append(head_norm_rope_b(zkb[lo:lo + DIFF_HD], kgain_b))
    for hd in range(DIFF_HEADS):
        k12 = jnp.concatenate(k_parts[2 * hd:2 * hd + 2], axis=0)
        kb_ref[0, :, hd * DIFF_V:(hd + 1) * DIFF_V] = k12.T.astype(BF16)
    zvb = proj_t(o_vb, o_g)
    for hd in range(DIFF_HEADS):
        vb_ref[0, hd * DIFF_VS:hd * DIFF_VS + DIFF_V, :] = zvb[hd * DIFF_V:(hd + 1) * DIFF_V].astype(BF16)
        vb_ref[0, hd * DIFF_VS + DIFF_V:(hd + 1) * DIFF_VS, :] = sum_rows

    g_chunk = 512
    for r0 in range(0, n_gate, g_chunk):
        g_ref[0, r0:r0 + g_chunk, :] = _sigmoid(proj_t(o_g + r0, o_g + r0 + g_chunk)).astype(BF16)


def _prep(h, mod, gain, pos3, w):
    bsz, s, d = h.shape
    tm = TOKEN_TILE
    n_in = w["win_t"].shape[0]
    wa, wb = MLA_HEADS * HEAD_PAD, DIFF_HEADS * DIFF_V
    n_gate = n_in - (MLA_Q_LORA + MLA_KV_LORA + MLA_ROPE + 3 * wb)

    def fm(rows):
        return (jax.ShapeDtypeStruct((bsz, rows, s), BF16),
                pl.BlockSpec((1, rows, tm), lambda b, i: (b, 0, i)))

    def tmaj(cols):
        return (jax.ShapeDtypeStruct((bsz, s, cols), BF16),
                pl.BlockSpec((1, tm, cols), lambda b, i: (b, i, 0)))

    outs = [fm(wa), tmaj(wa), fm(MLA_HEADS * MLA_VS), fm(wb), tmaj(wb), fm(DIFF_HEADS * DIFF_VS), fm(n_gate)]
    consts = [w["win_t"], w["wuq_t"], w["wukv_t"], w["qnorm"], w["kvnorm"],
              w["qgain_a"], w["kgain_a"], w["qgain_b"], w["kgain_b"], w["freq_a"], w["freq_b"]]
    in_specs = [
        pl.BlockSpec((1, tm, d), lambda b, i: (b, i, 0)),
        pl.BlockSpec((1, N_MOD, d), lambda b, i: (b, 0, 0)),
        _const_spec((1, d)),
        pl.BlockSpec((1, 1, tm), lambda b, i: (b, 0, i)),
    ] + [_const_spec(a.shape) for a in consts]
    return pl.pallas_call(
        _prep_kernel,
        grid=(bsz, s // tm),
        in_specs=in_specs,
        out_specs=[o[1] for o in outs],
        out_shape=[o[0] for o in outs],
        compiler_params=pltpu.CompilerParams(
            dimension_semantics=("arbitrary", "arbitrary"),
            vmem_limit_bytes=V7X_VMEM_LIMIT),
        name="prep",
    )(h, mod, gain.reshape(1, d), pos3, *consts)


def _normalise(acc, rows):
    return acc[:rows] * (1.0 / acc[rows:rows + 1])


def _attention_pipeline(s_len, n_streams, scores_fn, v_fn, finish_fn):
    units = [(q0, c0) for q0 in range(0, s_len, Q_TILE) for c0 in range(0, s_len, KEY_CHUNK)]
    n = len(units)
    s_cur = scores_fn(*units[0])
    m = [None] * n_streams
    acc = [None] * n_streams
    pending = None
    for i in range(n + 1):
        s_nxt = scores_fn(*units[i + 1]) if i + 1 < n else None
        if pending is not None:
            (q0, c0), alphas, probs = pending
            v = v_fn(c0)
            for t in range(n_streams):
                o = jnp.dot(v, probs[t], preferred_element_type=F32)
                acc[t] = o if alphas[t] is None else alphas[t] * acc[t] + o
            if c0 + KEY_CHUNK == s_len:
                finish_fn(q0, acc)
                acc = [None] * n_streams
            pending = None
        if i < n:
            q0, c0 = units[i]
            alphas, probs = [], []
            for t in range(n_streams):
                cmax = jnp.max(s_cur[t], axis=0, keepdims=True)
                if c0 == 0:
                    m[t] = cmax
                    alphas.append(None)
                else:
                    m_new = jnp.maximum(m[t], cmax)
                    alphas.append(jnp.exp2(m[t] - m_new))
                    m[t] = m_new
                probs.append(jnp.exp2(s_cur[t] - m[t]).astype(BF16))
            pending = (units[i], alphas, probs)
            s_cur = s_nxt


def _mla_kernel(k_ref, q_ref, v_ref, o_ref):
    def scores(q0, c0):
        return (jnp.dot(k_ref[0, c0:c0 + KEY_CHUNK, :], q_ref[0, :, q0:q0 + Q_TILE],
                        preferred_element_type=F32),)

    def values(c0):
        return v_ref[0, :, c0:c0 + KEY_CHUNK]

    def finish(q0, accs):
        o_ref[0, :, q0:q0 + Q_TILE] = _normalise(accs[0], MLA_V).astype(BF16)

    _attention_pipeline(k_ref.shape[1], 1, scores, values, finish)


def _mla_attention(ka, qa, va):
    bsz, s, _ = ka.shape
    return pl.pallas_call(
        _mla_kernel,
        grid=(bsz, MLA_HEADS),
        in_specs=[
            pl.BlockSpec((1, s, HEAD_PAD), lambda b, h: (b, 0, h)),
            pl.BlockSpec((1, HEAD_PAD, s), lambda b, h: (b, h, 0)),
            pl.BlockSpec((1, MLA_VS, s), lambda b, h: (b, h, 0)),
        ],
        out_specs=pl.BlockSpec((1, MLA_V, s), lambda b, h: (b, h, 0)),
        out_shape=jax.ShapeDtypeStruct((bsz, MLA_HEADS * MLA_V, s), BF16),
        compiler_params=pltpu.CompilerParams(
            dimension_semantics=("arbitrary", "arbitrary"),
            vmem_limit_bytes=V7X_VMEM_LIMIT),
        name="mla_attn",
    )(ka, qa, va)


def _diff_kernel(k_ref, q_ref, v_ref, lq1_ref, lk1_ref, lq2_ref, lk2_ref, subln_ref, o_ref, *,
                 lambda_init):
    lam = (jnp.exp(jnp.sum(lq1_ref[...] * lk1_ref[...], axis=-1, keepdims=True))
           - jnp.exp(jnp.sum(lq2_ref[...] * lk2_ref[...], axis=-1, keepdims=True))
           + lambda_init)
    subln = _lane_tile(subln_ref[...], Q_TILE)
    zeros = jnp.zeros((DIFF_HD, Q_TILE), BF16)

    def scores(q0, c0):
        q12 = q_ref[0, :, q0:q0 + Q_TILE]
        k12 = k_ref[0, c0:c0 + KEY_CHUNK, :]
        q1 = jnp.concatenate([q12[:DIFF_HD], zeros], axis=0)
        q2 = jnp.concatenate([zeros, q12[DIFF_HD:]], axis=0)
        return (jnp.dot(k12, q1, preferred_element_type=F32),
                jnp.dot(k12, q2, preferred_element_type=F32))

    def values(c0):
        return v_ref[0, :, c0:c0 + KEY_CHUNK]

    def finish(q0, accs):
        o = _normalise(accs[0], DIFF_V) - lam * _normalise(accs[1], DIFF_V)
        r = lax.rsqrt(jnp.mean(o * o, axis=0, keepdims=True) + NORM_EPS)
        o_ref[0, :, q0:q0 + Q_TILE] = ((o * r * subln) * (1.0 - lambda_init)).astype(BF16)

    _attention_pipeline(k_ref.shape[1], 2, scores, values, finish)


def _diff_attention(kb, qb, vb, lq1, lk1, lq2, lk2, subln, lambda_init):
    bsz, s, _ = kb.shape
    vec = _const_spec((1, DIFF_HD))
    return pl.pallas_call(
        functools.partial(_diff_kernel, lambda_init=lambda_init),
        grid=(bsz, DIFF_HEADS),
        in_specs=[
            pl.BlockSpec((1, s, DIFF_V), lambda b, h: (b, 0, h)),
            pl.BlockSpec((1, DIFF_V, s), lambda b, h: (b, h, 0)),
            pl.BlockSpec((1, DIFF_VS, s), lambda b, h: (b, h, 0)),
            vec, vec, vec, vec,
            _const_spec((DIFF_V, LANES)),
        ],
        out_specs=pl.BlockSpec((1, DIFF_V, s), lambda b, h: (b, h, 0)),
        out_shape=jax.ShapeDtypeStruct((bsz, DIFF_HEADS * DIFF_V, s), BF16),
        compiler_params=pltpu.CompilerParams(
            dimension_semantics=("arbitrary", "arbitrary"),
            vmem_limit_bytes=V7X_VMEM_LIMIT),
        name="diff_attn",
    )(kb, qb, vb, lq1, lk1, lq2, lk2, subln)


def _merge_kernel(oa_ref, ob_ref, g_ref, h_ref, mod_ref, woa_ref, wob_ref, wout_ref, o_ref):
    d = h_ref.shape[2]
    ya = jnp.dot(woa_ref[...], oa_ref[0], preferred_element_type=F32)
    yb = jnp.dot(wob_ref[...], ob_ref[0], preferred_element_type=F32)
    ga = g_ref[0, :d, :].astype(F32)
    gb = g_ref[0, d:, :].astype(F32)
    mix = (ga * ya + gb * yb).astype(BF16)
    y_t = jnp.dot(wout_ref[...], mix, preferred_element_type=F32)
    gate = mod_ref[0, 5:6, :]
    o_ref[0] = h_ref[0] + gate * y_t.T


def _merge(oa, ob, g, h, mod, woa_t, wob_t, wout_t):
    bsz, s, d = h.shape
    tm = TOKEN_TILE
    return pl.pallas_call(
        _merge_kernel,
        grid=(bsz, s // tm),
        in_specs=[
            pl.BlockSpec((1, oa.shape[1], tm), lambda b, i: (b, 0, i)),
            pl.BlockSpec((1, ob.shape[1], tm), lambda b, i: (b, 0, i)),
            pl.BlockSpec((1, g.shape[1], tm), lambda b, i: (b, 0, i)),
            pl.BlockSpec((1, tm, d), lambda b, i: (b, i, 0)),
            pl.BlockSpec((1, N_MOD, d), lambda b, i: (b, 0, 0)),
            _const_spec(woa_t.shape),
            _const_spec(wob_t.shape),
            _const_spec(wout_t.shape),
        ],
        out_specs=pl.BlockSpec((1, tm, d), lambda b, i: (b, i, 0)),
        out_shape=jax.ShapeDtypeStruct((bsz, s, d), F32),
        compiler_params=pltpu.CompilerParams(
            dimension_semantics=("arbitrary", "arbitrary"),
            vmem_limit_bytes=V7X_VMEM_LIMIT),
        name="merge",
    )(oa, ob, g, h, mod, woa_t, wob_t, wout_t)


def _lane_bcast(v, rows=None):
    n = v.shape[0]
    out = jnp.broadcast_to(v.astype(F32)[:, None], (n, LANES))
    if rows is not None and rows > n:
        out = jnp.pad(out, ((0, rows - n), (0, 0)))
    return out


def _rope_freqs(half, theta):
    return 1.0 / (theta ** (jnp.arange(half, dtype=F32) / half))


def _layer_weights(l, w_in, mla_q_norm, mla_w_uq, mla_kv_norm, mla_w_ukv, mla_q_gain, mla_k_gain,
                   diff_q_gain, diff_k_gain):
    wuq = mla_w_uq[l].reshape(MLA_Q_LORA, MLA_HEADS, MLA_QK)
    wuq = jnp.pad(wuq, ((0, 0), (0, 0), (0, HEAD_PAD - MLA_QK))).reshape(MLA_Q_LORA, -1)
    return {
        "win_t": w_in[l].T.astype(BF16),
        "wuq_t": wuq.T.astype(BF16),
        "wukv_t": mla_w_ukv[l].T.astype(BF16),
        "qnorm": _lane_bcast(mla_q_norm[l]),
        "kvnorm": _lane_bcast(mla_kv_norm[l]),
        "qgain_a": _lane_bcast(mla_q_gain[l], HEAD_PAD),
        "kgain_a": _lane_bcast(mla_k_gain[l], HEAD_PAD),
        "qgain_b": _lane_bcast(diff_q_gain[l]),
        "kgain_b": _lane_bcast(diff_k_gain[l]),
        "freq_a": _lane_bcast(_rope_freqs(MLA_ROPE // 2, MLA_THETA)),
        "freq_b": _lane_bcast(_rope_freqs(DIFF_ROT // 2, DIFF_THETA)),
    }


def kernel(x, c, positions, w_ada, b_ada, ffn1_norm, ffn1_w_gate, ffn1_w_up, ffn1_w_down, mix_norm, w_in, mla_q_norm, mla_w_uq, mla_kv_norm, mla_w_ukv, mla_q_gain, mla_k_gain, mla_w_o, diff_q_gain, diff_k_gain, diff_lambda_q1, diff_lambda_k1, diff_lambda_q2, diff_lambda_k2, diff_subln, diff_w_o, w_out, ffn2_norm, ffn2_w_gate, ffn2_w_up, ffn2_w_down, final_norm):
    bsz, s, d = x.shape
    depth = w_ada.shape[0]
    pos3 = positions.reshape(bsz, 1, s)
    h = x
    for l in range(depth):
        lambda_init = 0.8 - 0.6 * math.exp(-0.3 * l)
        mod = _ada(c, w_ada[l], b_ada[l]).reshape(bsz, N_MOD, d)

        h = _ffn(h, mod, ffn1_norm[l], ffn1_w_gate[l].astype(BF16), ffn1_w_up[l].astype(BF16),
                 ffn1_w_down[l].astype(BF16), sub=0)

        w = _layer_weights(l, w_in, mla_q_norm, mla_w_uq, mla_kv_norm, mla_w_ukv, mla_q_gain,
                           mla_k_gain, diff_q_gain, diff_k_gain)
        qa, ka, va, qb, kb, vb, g = _prep(h, mod, mix_norm[l], pos3, w)
        oa = _mla_attention(ka, qa, va)
        ob = _diff_attention(kb, qb, vb,
                             diff_lambda_q1[l].reshape(1, -1), diff_lambda_k1[l].reshape(1, -1),
                             diff_lambda_q2[l].reshape(1, -1), diff_lambda_k2[l].reshape(1, -1),
                             _lane_bcast(diff_subln[l]), lambda_init)
        h = _merge(oa, ob, g, h, mod, mla_w_o[l].T.astype(BF16), diff_w_o[l].T.astype(BF16),
                   w_out[l].T.astype(BF16))

        h = _ffn(h, mod, ffn2_norm[l], ffn2_w_gate[l].astype(BF16), ffn2_w_up[l].astype(BF16),
                 ffn2_w_down[l].astype(BF16), sub=2, final_gain=final_norm[l])
    return h
```

```python
import functools
import math

import jax
import jax.numpy as jnp
from jax import lax
from jax.experimental import pallas as pl
from jax.experimental.pallas import tpu as pltpu

F32 = jnp.float32
BF16 = jnp.bfloat16

NORM_EPS = 1e-6
N_MOD = 9

MLA_HEADS = 8
MLA_NOPE = 64
MLA_ROPE = 32
MLA_QK = MLA_NOPE + MLA_ROPE
MLA_V = 64
MLA_Q_LORA = 384
MLA_KV_LORA = 256
MLA_THETA = 10000.0
DIFF_HEADS = 4
DIFF_HD = 64
DIFF_V = 2 * DIFF_HD
DIFF_THETA = 500000.0
DIFF_ROT = DIFF_HD // 4

LANES = 128
HEAD_PAD = 128
SUM_ROWS = 16
MLA_VS = MLA_V + SUM_ROWS
DIFF_VS = DIFF_V + SUM_ROWS
LOG2E = math.log2(math.e)
V7X_VMEM_LIMIT = 56 * 1024 * 1024

TOKEN_TILE = 512
FFN_TOKEN_TILE = 1024
FFN_SUB_TILE = 512
MLA_Q_TILE = 512
DIFF_Q_TILE = 512
MLA_KEY_CHUNK = 256
DIFF_KEY_CHUNK = 512
MLA_QK_GROUP = 1
DIFF_QK_GROUP = 1
FF_CHUNK = 256
ADA_COL_BLOCK = 1024


def _sigmoid(x):
    return 1.0 / (1.0 + jnp.exp(-x))


def _rms_rows(x, gain):
    ms = jnp.mean(x * x, axis=-1, keepdims=True)
    return x * lax.rsqrt(ms + NORM_EPS) * gain


def _lane_tile(g, width):
    return jnp.tile(g, (1, width // LANES))


def _const_spec(shape):
    return pl.BlockSpec(shape, lambda *_: (0,) * len(shape), pipeline_mode=pl.Buffered(1))


def _ada_kernel(c_ref, w_ref, b_ref, o_ref):
    c = c_ref[...]
    cond = c * _sigmoid(c)
    o_ref[...] = jnp.dot(cond, w_ref[...], preferred_element_type=F32,
                         precision=lax.Precision.HIGHEST) + b_ref[...]


def _ada(c, w_ada, b_ada):
    bsz, d = c.shape
    cols = w_ada.shape[1]
    return pl.pallas_call(
        _ada_kernel,
        grid=(cols // ADA_COL_BLOCK,),
        in_specs=[
            pl.BlockSpec((bsz, d), lambda j: (0, 0)),
            pl.BlockSpec((d, ADA_COL_BLOCK), lambda j: (0, j)),
            pl.BlockSpec((1, ADA_COL_BLOCK), lambda j: (0, j)),
        ],
        out_specs=pl.BlockSpec((bsz, ADA_COL_BLOCK), lambda j: (0, j)),
        out_shape=jax.ShapeDtypeStruct((bsz, cols), F32),
        compiler_params=pltpu.CompilerParams(dimension_semantics=("arbitrary",)),
        name="ada",
    )(c, w_ada, b_ada.reshape(1, cols))


def _adaln(x, gain, mod_ref, sub):
    shift = mod_ref[0, 3 * sub:3 * sub + 1, :]
    scale = mod_ref[0, 3 * sub + 1:3 * sub + 2, :]
    return (_rms_rows(x, gain) * (1.0 + scale) + shift).astype(BF16)


def _ffn_kernel(*refs, sub, n_given, emit_next, final):
    refs = list(refs)
    x_ref = refs.pop(0)
    n_ref = refs.pop(0) if n_given else None
    mod_ref = refs.pop(0)
    gain_ref = None if n_given else refs.pop(0)
    wg_ref, wu_ref, wd_ref = refs.pop(0), refs.pop(0), refs.pop(0)
    next_gain_ref = refs.pop(0) if emit_next else None
    fgain_ref = refs.pop(0) if final else None
    o_ref = refs.pop(0)
    n_next_ref = refs.pop(0) if emit_next else None
    a_ref, = refs

    d_ff = wg_ref.shape[1]
    gate = mod_ref[0, 3 * sub + 2:3 * sub + 3, :]
    halves = [slice(r0, r0 + FFN_SUB_TILE) for r0 in range(0, x_ref.shape[1], FFN_SUB_TILE)]

    def up(rows):
        n = n_ref[0, rows, :] if n_given else _adaln(x_ref[0, rows, :], gain_ref[...], mod_ref, sub)
        for c0 in range(0, d_ff, FF_CHUNK):
            g = jnp.dot(n, wg_ref[:, c0:c0 + FF_CHUNK], preferred_element_type=F32)
            u = jnp.dot(n, wu_ref[:, c0:c0 + FF_CHUNK], preferred_element_type=F32)
            a_ref[rows, c0:c0 + FF_CHUNK] = ((g * _sigmoid(g)) * u).astype(BF16)

    def down(rows):
        f = jnp.dot(a_ref[rows, :], wd_ref[...], preferred_element_type=F32)
        h = x_ref[0, rows, :] + (0.5 * gate) * f
        if final:
            h = _rms_rows(h, fgain_ref[...])
        o_ref[0, rows, :] = h
        if emit_next:
            n_next_ref[0, rows, :] = _adaln(h, next_gain_ref[...], mod_ref, sub + 1)

    up(halves[0])
    for prev, cur in zip(halves[:-1], halves[1:]):
        up(cur)
        down(prev)
    down(halves[-1])


def _ffn(h, mod, wg, wu, wd, *, sub, gain=None, n=None, next_gain=None, final_gain=None):
    bsz, s, d = h.shape
    d_ff = wg.shape[1]
    tm = FFN_TOKEN_TILE
    tile = pl.BlockSpec((1, tm, d), lambda b, i: (b, i, 0))
    in_specs, args = [tile], [h]
    if n is not None:
        in_specs.append(tile)
        args.append(n)
    in_specs.append(pl.BlockSpec((1, N_MOD, d), lambda b, i: (b, 0, 0)))
    args.append(mod)
    if n is None:
        in_specs.append(_const_spec((1, d)))
        args.append(gain.reshape(1, d))
    in_specs += [_const_spec((d, d_ff)), _const_spec((d, d_ff)), _const_spec((d_ff, d))]
    args += [wg, wu, wd]
    for extra in (next_gain, final_gain):
        if extra is not None:
            in_specs.append(_const_spec((1, d)))
            args.append(extra.reshape(1, d))
    out_specs, out_shape = [tile], [jax.ShapeDtypeStruct((bsz, s, d), F32)]
    if next_gain is not None:
        out_specs.append(tile)
        out_shape.append(jax.ShapeDtypeStruct((bsz, s, d), BF16))
    outs = pl.pallas_call(
        functools.partial(_ffn_kernel, sub=sub, n_given=n is not None,
                          emit_next=next_gain is not None, final=final_gain is not None),
        grid=(bsz, s // tm),
        in_specs=in_specs,
        out_specs=out_specs,
        out_shape=out_shape,
        scratch_shapes=[pltpu.VMEM((tm, d_ff), BF16)],
        compiler_params=pltpu.CompilerParams(
            dimension_semantics=("arbitrary", "arbitrary"),
            vmem_limit_bytes=V7X_VMEM_LIMIT),
        name="ffn%d" % sub,
    )(*args)
    return outs if next_gain is not None else outs[0]


def _rope_rows(x1, x2, cos, sin):
    return x1 * cos - x2 * sin, x2 * cos + x1 * sin


def _prep_kernel(n_ref, pos_ref, win_ref, wuq_ref, wukv_ref,
                 qnorm_ref, kvnorm_ref, qgain_a_ref, kgain_a_ref, qgain_b_ref, kgain_b_ref,
                 freq_a_ref, freq_b_ref,
                 qa_ref, ka_ref, va_ref, qb_ref, kb_ref, vb_ref, g_ref):
    tm = n_ref.shape[1]
    n = n_ref[0]

    def proj_t(r0, r1):
        return lax.dot_general(win_ref[r0:r1, :], n, (((1,), (1,)), ((), ())),
                               preferred_element_type=F32)

    o_q, o_kv, o_kr = 0, MLA_Q_LORA, MLA_Q_LORA + MLA_KV_LORA
    o_qb = o_kr + MLA_ROPE
    w_b = DIFF_HEADS * DIFF_V
    o_kb, o_vb, o_g = o_qb + w_b, o_qb + 2 * w_b, o_qb + 3 * w_b
    g_chunk = g_ref.shape[1] // 4

    def emit_gates(i):
        r0 = i * g_chunk
        g_ref[0, r0:r0 + g_chunk, :] = _sigmoid(proj_t(o_g + r0, o_g + r0 + g_chunk)).astype(BF16)

    z_a = proj_t(o_q, o_qb)
    emit_gates(0)
    zq, zkv, kr = z_a[o_q:o_kv], z_a[o_kv:o_kr], z_a[o_kr:o_qb]
    rq = lax.rsqrt(jnp.mean(zq * zq, axis=0, keepdims=True) + NORM_EPS)
    zqn = (zq * rq * _lane_tile(qnorm_ref[...], tm)).astype(BF16)
    q_all = jnp.dot(wuq_ref[...], zqn, preferred_element_type=F32)
    rkv = lax.rsqrt(jnp.mean(zkv * zkv, axis=0, keepdims=True) + NORM_EPS)
    zkvn = (zkv * rkv * _lane_tile(kvnorm_ref[...], tm)).astype(BF16)
    kv_all = jnp.dot(wukv_ref[...], zkvn, preferred_element_type=F32)
    emit_gates(1)

    pos = pos_ref[0].astype(F32)
    ang_a = pos * _lane_tile(freq_a_ref[...], tm)
    cos_a, sin_a = jnp.cos(ang_a), jnp.sin(ang_a)
    ang_b = pos * _lane_tile(freq_b_ref[...], tm)
    cos_b, sin_b = jnp.cos(ang_b), jnp.sin(ang_b)

    qgain_a = _lane_tile(qgain_a_ref[...], tm)
    kgain_a = _lane_tile(kgain_a_ref[...], tm)
    q_scale = LOG2E / math.sqrt(MLA_QK)
    pad_rows = jnp.zeros((HEAD_PAD - MLA_QK, tm), F32)
    sum_rows = (lax.broadcasted_iota(jnp.int32, (SUM_ROWS, tm), 0) == 0).astype(BF16)
    half = MLA_ROPE // 2

    def head_norm_rope_a(xh, gain):
        r = lax.rsqrt(jnp.sum(xh * xh, axis=0, keepdims=True) * (1.0 / MLA_QK) + NORM_EPS)
        xh = xh * r * gain
        r1, r2 = _rope_rows(xh[MLA_NOPE:MLA_NOPE + half], xh[MLA_NOPE + half:MLA_QK], cos_a, sin_a)
        return jnp.concatenate([xh[:MLA_NOPE], r1, r2, xh[MLA_QK:]], axis=0)

    zqb = proj_t(o_qb, o_kb)
    zkb = proj_t(o_kb, o_vb)
    for hd in range(MLA_HEADS):
        lo = hd * HEAD_PAD
        qh = head_norm_rope_a(q_all[lo:lo + HEAD_PAD], qgain_a) * q_scale
        qa_ref[0, lo:lo + HEAD_PAD, :] = qh.astype(BF16)
        kvh = kv_all[lo:lo + HEAD_PAD]
        va_ref[0, hd * MLA_VS:hd * MLA_VS + MLA_V, :] = kvh[MLA_NOPE:].astype(BF16)
        va_ref[0, hd * MLA_VS + MLA_V:(hd + 1) * MLA_VS, :] = sum_rows
        kh = jnp.concatenate([kvh[:MLA_NOPE], kr, pad_rows], axis=0)
        kh = head_norm_rope_a(kh, kgain_a)
        ka_ref[0, :, lo:lo + HEAD_PAD] = kh.T.astype(BF16)
    emit_gates(2)

    qgain_b = _lane_tile(qgain_b_ref[...], tm)
    kgain_b = _lane_tile(kgain_b_ref[...], tm)
    qb_scale = LOG2E / math.sqrt(DIFF_HD)
    hb = DIFF_ROT // 2

    def head_norm_rope_b(xh, gain):
        r = lax.rsqrt(jnp.mean(xh * xh, axis=0, keepdims=True) + NORM_EPS)
        xh = xh * r * gain
        r1, r2 = _rope_rows(xh[:hb], xh[hb:DIFF_ROT], cos_b, sin_b)
        return jnp.concatenate([r1, r2, xh[DIFF_ROT:]], axis=0)

    zvb = proj_t(o_vb, o_g)
    k_parts = []
    for blk in range(2 * DIFF_HEADS):
        lo = blk * DIFF_HD
        qh = head_norm_rope_b(zqb[lo:lo + DIFF_HD], qgain_b) * qb_scale
        qb_ref[0, lo:lo + DIFF_HD, :] = qh.astype(BF16)
        k_parts.append(head_norm_rope_b(zkb[lo:lo + DIFF_HD], kgain_b))
    for hd in range(DIFF_HEADS):
        k12 = jnp.concatenate(k_parts[2 * hd:2 * hd + 2], axis=0)
        kb_ref[0, :, hd * DIFF_V:(hd + 1) * DIFF_V] = k12.T.astype(BF16)
        vb_ref[0, hd * DIFF_VS:hd * DIFF_VS + DIFF_V, :] = zvb[hd * DIFF_V:(hd + 1) * DIFF_V].astype(BF16)
        vb_ref[0, hd * DIFF_VS + DIFF_V:(hd + 1) * DIFF_VS, :] = sum_rows
    emit_gates(3)


def _prep(n, pos3, w):
    bsz, s, d = n.shape
    tm = TOKEN_TILE
    n_in = w["win_t"].shape[0]
    wa, wb = MLA_HEADS * HEAD_PAD, DIFF_HEADS * DIFF_V
    n_gate = n_in - (MLA_Q_LORA + MLA_KV_LORA + MLA_ROPE + 3 * wb)

    def fm(rows):
        return (jax.ShapeDtypeStruct((bsz, rows, s), BF16),
                pl.BlockSpec((1, rows, tm), lambda b, i: (b, 0, i)))

    def tmaj(cols):
        return (jax.ShapeDtypeStruct((bsz, s, cols), BF16),
                pl.BlockSpec((1, tm, cols), lambda b, i: (b, i, 0)))

    outs = [fm(wa), tmaj(wa), fm(MLA_HEADS * MLA_VS), fm(wb), tmaj(wb), fm(DIFF_HEADS * DIFF_VS), fm(n_gate)]
    consts = [w["win_t"], w["wuq_t"], w["wukv_t"], w["qnorm"], w["kvnorm"],
              w["qgain_a"], w["kgain_a"], w["qgain_b"], w["kgain_b"], w["freq_a"], w["freq_b"]]
    in_specs = [
        pl.BlockSpec((1, tm, d), lambda b, i: (b, i, 0)),
        pl.BlockSpec((1, 1, tm), lambda b, i: (b, 0, i)),
    ] + [_const_spec(a.shape) for a in consts]
    return pl.pallas_call(
        _prep_kernel,
        grid=(bsz, s // tm),
        in_specs=in_specs,
        out_specs=[o[1] for o in outs],
        out_shape=[o[0] for o in outs],
        compiler_params=pltpu.CompilerParams(
            dimension_semantics=("arbitrary", "arbitrary"),
            vmem_limit_bytes=V7X_VMEM_LIMIT),
        name="prep",
    )(n, pos3, *consts)


def _normalise(acc, rows):
    return acc[:rows] * (1.0 / acc[rows:rows + 1])


def _attention_pipeline(s_len, tq, kc, qk_group, n_streams, scores_fn, v_fn, finish_fn):
    units = [(q0, c0) for q0 in range(0, s_len, tq) for c0 in range(0, s_len, kc)]
    streams = range(n_streams)
    n = len(units)
    group = [None]

    def unit_scores(q0, c0):
        g0 = c0 % (kc * qk_group)
        if g0 == 0:
            group[0] = [scores_fn(q0, c0, kc * qk_group, t) for t in streams]
        return [sg[g0:g0 + kc] for sg in group[0]]

    s_cur = unit_scores(*units[0])
    m = [None] * n_streams
    acc = [None] * n_streams
    pending = None
    for i in range(n + 1):
        s_nxt = unit_scores(*units[i + 1]) if i + 1 < n else None
        if pending is not None:
            (q0, c0), alphas, probs = pending
            v = v_fn(c0)
            for t in streams:
                o = jnp.dot(v, probs[t], preferred_element_type=F32)
                acc[t] = o if alphas[t] is None else alphas[t] * acc[t] + o
            if c0 + kc == s_len:
                finish_fn(q0, acc)
                acc = [None] * n_streams
            pending = None
        if i < n:
            q0, c0 = units[i]
            alphas, probs = [], []
            for t in streams:
                cmax = jnp.max(s_cur[t], axis=0, keepdims=True)
                if c0 == 0:
                    m[t] = cmax
                    alphas.append(None)
                else:
                    m_new = jnp.maximum(m[t], cmax)
                    alphas.append(jnp.exp2(m[t] - m_new))
                    m[t] = m_new
                probs.append(jnp.exp2(s_cur[t] - m[t]).astype(BF16))
            pending = (units[i], alphas, probs)
            s_cur = s_nxt


def _mla_kernel(k_ref, q_ref, v_ref, o_ref):
    def scores(q0, c0, rows, t):
        return jnp.dot(k_ref[0, c0:c0 + rows, :], q_ref[0, :, q0:q0 + MLA_Q_TILE],
                       preferred_element_type=F32)

    def values(c0):
        return v_ref[0, :, c0:c0 + MLA_KEY_CHUNK]

    def finish(q0, accs):
        o_ref[0, :, q0:q0 + MLA_Q_TILE] = _normalise(accs[0], MLA_V).astype(BF16)

    _attention_pipeline(k_ref.shape[1], MLA_Q_TILE, MLA_KEY_CHUNK, MLA_QK_GROUP, 1,
                        scores, values, finish)


def _mla_attention(ka, qa, va):
    bsz, s, _ = ka.shape
    return pl.pallas_call(
        _mla_kernel,
        grid=(bsz, MLA_HEADS),
        in_specs=[
            pl.BlockSpec((1, s, HEAD_PAD), lambda b, h: (b, 0, h)),
            pl.BlockSpec((1, HEAD_PAD, s), lambda b, h: (b, h, 0)),
            pl.BlockSpec((1, MLA_VS, s), lambda b, h: (b, h, 0)),
        ],
        out_specs=pl.BlockSpec((1, MLA_V, s), lambda b, h: (b, h, 0)),
        out_shape=jax.ShapeDtypeStruct((bsz, MLA_HEADS * MLA_V, s), BF16),
        compiler_params=pltpu.CompilerParams(
            dimension_semantics=("arbitrary", "arbitrary"),
            vmem_limit_bytes=V7X_VMEM_LIMIT),
        name="mla_attn",
    )(ka, qa, va)


def _diff_kernel(k_ref, q_ref, v_ref, lq1_ref, lk1_ref, lq2_ref, lk2_ref, subln_ref, o_ref, *,
                 lambda_init):
    lam = (jnp.exp(jnp.sum(lq1_ref[...] * lk1_ref[...], axis=-1, keepdims=True))
           - jnp.exp(jnp.sum(lq2_ref[...] * lk2_ref[...], axis=-1, keepdims=True))
           + lambda_init)
    subln = _lane_tile(subln_ref[...], DIFF_Q_TILE)
    zeros = jnp.zeros((DIFF_HD, DIFF_Q_TILE), BF16)

    def scores(q0, c0, rows, t):
        q12 = q_ref[0, :, q0:q0 + DIFF_Q_TILE]
        k12 = k_ref[0, c0:c0 + rows, :]
        if t == 0:
            q = jnp.concatenate([q12[:DIFF_HD], zeros], axis=0)
        else:
            q = jnp.concatenate([zeros, q12[DIFF_HD:]], axis=0)
        return jnp.dot(k12, q, preferred_element_type=F32)

    def values(c0):
        return v_ref[0, :, c0:c0 + DIFF_KEY_CHUNK]

    def finish(q0, accs):
        o = _normalise(accs[0], DIFF_V) - lam * _normalise(accs[1], DIFF_V)
        r = lax.rsqrt(jnp.mean(o * o, axis=0, keepdims=True) + NORM_EPS)
        o_ref[0, :, q0:q0 + DIFF_Q_TILE] = ((o * r * subln) * (1.0 - lambda_init)).astype(BF16)

    _attention_pipeline(k_ref.shape[1], DIFF_Q_TILE, DIFF_KEY_CHUNK, DIFF_QK_GROUP, 2,
                        scores, values, finish)


def _diff_attention(kb, qb, vb, lq1, lk1, lq2, lk2, subln, lambda_init):
    bsz, s, _ = kb.shape
    vec = _const_spec((1, DIFF_HD))
    return pl.pallas_call(
        functools.partial(_diff_kernel, lambda_init=lambda_init),
        grid=(bsz, DIFF_HEADS),
        in_specs=[
            pl.BlockSpec((1, s, DIFF_V), lambda b, h: (b, 0, h)),
            pl.BlockSpec((1, DIFF_V, s), lambda b, h: (b, h, 0)),
            pl.BlockSpec((1, DIFF_VS, s), lambda b, h: (b, h, 0)),
            vec, vec, vec, vec,
            _const_spec((DIFF_V, LANES)),
        ],
        out_specs=pl.BlockSpec((1, DIFF_V, s), lambda b, h: (b, h, 0)),
        out_shape=jax.ShapeDtypeStruct((bsz, DIFF_HEADS * DIFF_V, s), BF16),
        compiler_params=pltpu.CompilerParams(
            dimension_semantics=("arbitrary", "arbitrary"),
            vmem_limit_bytes=V7X_VMEM_LIMIT),
        name="diff_attn",
    )(kb, qb, vb, lq1, lk1, lq2, lk2, subln)


def _merge_kernel(oa_ref, ob_ref, g_ref, h_ref, mod_ref, woa_ref, wob_ref, wout_ref, o_ref):
    d = h_ref.shape[2]
    ya = jnp.dot(woa_ref[...], oa_ref[0], preferred_element_type=F32)
    yb = jnp.dot(wob_ref[...], ob_ref[0], preferred_element_type=F32)
    ga = g_ref[0, :d, :].astype(F32)
    gb = g_ref[0, d:, :].astype(F32)
    mix = (ga * ya + gb * yb).astype(BF16)
    y_t = jnp.dot(wout_ref[...], mix, preferred_element_type=F32)
    gate = mod_ref[0, 5:6, :]
    o_ref[0] = h_ref[0] + gate * y_t.T


def _merge(oa, ob, g, h, mod, woa_t, wob_t, wout_t):
    bsz, s, d = h.shape
    tm = TOKEN_TILE
    return pl.pallas_call(
        _merge_kernel,
        grid=(bsz, s // tm),
        in_specs=[
            pl.BlockSpec((1, oa.shape[1], tm), lambda b, i: (b, 0, i)),
            pl.BlockSpec((1, ob.shape[1], tm), lambda b, i: (b, 0, i)),
            pl.BlockSpec((1, g.shape[1], tm), lambda b, i: (b, 0, i)),
            pl.BlockSpec((1, tm, d), lambda b, i: (b, i, 0)),
            pl.BlockSpec((1, N_MOD, d), lambda b, i: (b, 0, 0)),
            _const_spec(woa_t.shape),
            _const_spec(wob_t.shape),
            _const_spec(wout_t.shape),
        ],
        out_specs=pl.BlockSpec((1, tm, d), lambda b, i: (b, i, 0)),
        out_shape=jax.ShapeDtypeStruct((bsz, s, d), F32),
        compiler_params=pltpu.CompilerParams(
            dimension_semantics=("arbitrary", "arbitrary"),
            vmem_limit_bytes=V7X_VMEM_LIMIT),
        name="merge",
    )(oa, ob, g, h, mod, woa_t, wob_t, wout_t)


def _lane_bcast(v, rows=None):
    n = v.shape[0]
    out = jnp.broadcast_to(v.astype(F32)[:, None], (n, LANES))
    if rows is not None and rows > n:
        out = jnp.pad(out, ((0, rows - n), (0, 0)))
    return out


def _rope_freqs(half, theta):
    return 1.0 / (theta ** (jnp.arange(half, dtype=F32) / half))


def _layer_weights(l, w_in, mla_q_norm, mla_w_uq, mla_kv_norm, mla_w_ukv, mla_q_gain, mla_k_gain,
                   diff_q_gain, diff_k_gain):
    wuq = mla_w_uq[l].reshape(MLA_Q_LORA, MLA_HEADS, MLA_QK)
    wuq = jnp.pad(wuq, ((0, 0), (0, 0), (0, HEAD_PAD - MLA_QK))).reshape(MLA_Q_LORA, -1)
    return {
        "win_t": w_in[l].T.astype(BF16),
        "wuq_t": wuq.T.astype(BF16),
        "wukv_t": mla_w_ukv[l].T.astype(BF16),
        "qnorm": _lane_bcast(mla_q_norm[l]),
        "kvnorm": _lane_bcast(mla_kv_norm[l]),
        "qgain_a": _lane_bcast(mla_q_gain[l], HEAD_PAD),
        "kgain_a": _lane_bcast(mla_k_gain[l], HEAD_PAD),
        "qgain_b": _lane_bcast(diff_q_gain[l]),
        "kgain_b": _lane_bcast(diff_k_gain[l]),
        "freq_a": _lane_bcast(_rope_freqs(MLA_ROPE // 2, MLA_THETA)),
        "freq_b": _lane_bcast(_rope_freqs(DIFF_ROT // 2, DIFF_THETA)),
    }


def kernel(x, c, positions, w_ada, b_ada, ffn1_norm, ffn1_w_gate, ffn1_w_up, ffn1_w_down, mix_norm, w_in, mla_q_norm, mla_w_uq, mla_kv_norm, mla_w_ukv, mla_q_gain, mla_k_gain, mla_w_o, diff_q_gain, diff_k_gain, diff_lambda_q1, diff_lambda_k1, diff_lambda_q2, diff_lambda_k2, diff_subln, diff_w_o, w_out, ffn2_norm, ffn2_w_gate, ffn2_w_up, ffn2_w_down, final_norm):
    bsz, s, d = x.shape
    depth = w_ada.shape[0]
    pos3 = positions.reshape(bsz, 1, s)
    h = x
    for l in range(depth):
        lambda_init = 0.8 - 0.6 * math.exp(-0.3 * l)
        mod = _ada(c, w_ada[l], b_ada[l]).reshape(bsz, N_MOD, d)

        h, n_mix = _ffn(h, mod, ffn1_w_gate[l].astype(BF16), ffn1_w_up[l].astype(BF16),
                        ffn1_w_down[l].astype(BF16), sub=0, gain=ffn1_norm[l],
                        next_gain=mix_norm[l])

        w = _layer_weights(l, w_in, mla_q_norm, mla_w_uq, mla_kv_norm, mla_w_ukv, mla_q_gain,
                           mla_k_gain, diff_q_gain, diff_k_gain)
        qa, ka, va, qb, kb, vb, g = _prep(n_mix, pos3, w)
        oa = _mla_attention(ka, qa, va)
        ob = _diff_attention(kb, qb, vb,
                             diff_lambda_q1[l].reshape(1, -1), diff_lambda_k1[l].reshape(1, -1),
                             diff_lambda_q2[l].reshape(1, -1), diff_lambda_k2[l].reshape(1, -1),
                             _lane_bcast(diff_subln[l]), lambda_init)
        h = _merge(oa, ob, g, h, mod, mla_w_o[l].T.astype(BF16), diff_w_o[l].T.astype(BF16),
                   w_out[l].T.astype(BF16))

        h = _ffn(h, mod, ffn2_w_gate[l].astype(BF16), ffn2_w_up[l].astype(BF16),
                 ffn2_w_down[l].astype(BF16), sub=2, gain=ffn2_norm[l],
                 final_gain=final_norm[l])
    return h
```

```python
import functools
import math

import jax
import jax.numpy as jnp
from jax import lax
from jax.experimental import pallas as pl
from jax.experimental.pallas import tpu as pltpu

F32 = jnp.float32
BF16 = jnp.bfloat16

NORM_EPS = 1e-6
N_MOD = 9

MLA_HEADS = 8
MLA_NOPE = 64
MLA_ROPE = 32
MLA_QK = MLA_NOPE + MLA_ROPE
MLA_V = 64
MLA_Q_LORA = 384
MLA_KV_LORA = 256
MLA_THETA = 10000.0
DIFF_HEADS = 4
DIFF_HD = 64
DIFF_V = 2 * DIFF_HD
DIFF_THETA = 500000.0
DIFF_ROT = DIFF_HD // 4

LANES = 128
HEAD_PAD = 128
SUM_ROWS = 16
MLA_VS = MLA_V + SUM_ROWS
DIFF_VS = DIFF_V + SUM_ROWS
LOG2E = math.log2(math.e)
V7X_VMEM_LIMIT = 56 * 1024 * 1024

PREP_TOKEN_TILE = 1024
PREP_SUB_TILE = 512
FFN_TOKEN_TILE = 1024
FFN_SUB_TILE = 512
MERGE_TOKEN_TILE = 1024
MERGE_SUB_TILE = 512
MLA_Q_TILE = 512
DIFF_Q_TILE = 512
MLA_KEY_CHUNK = 256
DIFF_KEY_CHUNK = 512
MLA_QK_GROUP = 1
DIFF_QK_GROUP = 1
FF_CHUNK = 256
LOAD_CHUNK = 128
LOAD_SLOTS = 4
ADA_COL_BLOCK = 1024


def _sigmoid(x):
    return 1.0 / (1.0 + jnp.exp(-x))


def _rms_rows(x, gain):
    ms = jnp.mean(x * x, axis=-1, keepdims=True)
    return x * lax.rsqrt(ms + NORM_EPS) * gain


def _lane_tile(g, width):
    return jnp.tile(g, (1, width // LANES))


def _const_spec(shape):
    return pl.BlockSpec(shape, lambda *_: (0,) * len(shape), pipeline_mode=pl.Buffered(1))


def _ada_kernel(c_ref, w_ref, b_ref, o_ref):
    c = c_ref[...]
    bsz = c.shape[0]
    cond = c * _sigmoid(c)
    c_hi = cond.astype(BF16).astype(F32)
    lhs = jnp.concatenate([c_hi, cond - c_hi], axis=0).astype(BF16)
    w = w_ref[...]
    w_hi = w.astype(BF16)
    w_lo = (w - w_hi.astype(F32)).astype(BF16)
    a = jnp.dot(lhs, w_hi, preferred_element_type=F32)
    b = jnp.dot(lhs, w_lo, preferred_element_type=F32)
    o_ref[...] = a[:bsz] + a[bsz:] + b[:bsz] + b_ref[...]


def _ada(c, w_ada, b_ada):
    bsz, d = c.shape
    cols = w_ada.shape[1]
    return pl.pallas_call(
        _ada_kernel,
        grid=(cols // ADA_COL_BLOCK,),
        in_specs=[
            pl.BlockSpec((bsz, d), lambda j: (0, 0)),
            pl.BlockSpec((d, ADA_COL_BLOCK), lambda j: (0, j)),
            pl.BlockSpec((1, ADA_COL_BLOCK), lambda j: (0, j)),
        ],
        out_specs=pl.BlockSpec((bsz, ADA_COL_BLOCK), lambda j: (0, j)),
        out_shape=jax.ShapeDtypeStruct((bsz, cols), F32),
        compiler_params=pltpu.CompilerParams(dimension_semantics=("arbitrary",)),
        name="ada",
    )(c, w_ada, b_ada.reshape(1, cols))


def _adaln(x, gain, mod_ref, sub, dtype=BF16):
    shift = mod_ref[0, 3 * sub:3 * sub + 1, :]
    scale = mod_ref[0, 3 * sub + 1:3 * sub + 2, :]
    return (_rms_rows(x, gain) * (1.0 + scale) + shift).astype(dtype)


def _load_ffn_weights(wg_hbm, wu_hbm, wd_hbm, wg_s, wu_s, wd_s, stage_c, stage_r, sem):
    d_ff = wg_s.shape[1]
    jobs = []
    for c0 in range(0, d_ff, LOAD_CHUNK):
        cols = slice(c0, c0 + LOAD_CHUNK)
        jobs.append((wg_hbm.at[:, cols], stage_c, wg_s.at[:, cols]))
        jobs.append((wu_hbm.at[:, cols], stage_c, wu_s.at[:, cols]))
    for r0 in range(0, d_ff, LOAD_CHUNK):
        rows = slice(r0, r0 + LOAD_CHUNK)
        jobs.append((wd_hbm.at[rows, :], stage_r, wd_s.at[rows, :]))

    def copy(i):
        src, stage, _ = jobs[i]
        return pltpu.make_async_copy(src, stage.at[i % LOAD_SLOTS], sem.at[i % LOAD_SLOTS])

    ahead = LOAD_SLOTS - 1
    for i in range(min(ahead, len(jobs))):
        copy(i).start()
    for i, (_, stage, dst) in enumerate(jobs):
        if i + ahead < len(jobs):
            copy(i + ahead).start()
        copy(i).wait()
        dst[...] = stage[i % LOAD_SLOTS].astype(BF16)


def _ffn_kernel(*refs, sub, emit_next, final):
    refs = list(refs)
    x_ref, mod_ref, gain_ref = refs.pop(0), refs.pop(0), refs.pop(0)
    w_hbm = [refs.pop(0) for _ in range(3)]
    next_gain_ref = refs.pop(0) if emit_next else None
    fgain_ref = refs.pop(0) if final else None
    o_ref = refs.pop(0)
    n_next_ref = refs.pop(0) if emit_next else None
    a_ref, wg_ref, wu_ref, wd_ref, stage_c, stage_r, sem = refs

    @pl.when((pl.program_id(0) == 0) & (pl.program_id(1) == 0))
    def _():
        _load_ffn_weights(*w_hbm, wg_ref, wu_ref, wd_ref, stage_c, stage_r, sem)

    d_ff = wg_ref.shape[1]
    gate = mod_ref[0, 3 * sub + 2:3 * sub + 3, :]
    halves = [slice(r0, r0 + FFN_SUB_TILE) for r0 in range(0, x_ref.shape[1], FFN_SUB_TILE)]

    def up(rows):
        n = _adaln(x_ref[0, rows, :], gain_ref[...], mod_ref, sub)
        for c0 in range(0, d_ff, FF_CHUNK):
            g = jnp.dot(n, wg_ref[:, c0:c0 + FF_CHUNK], preferred_element_type=F32)
            u = jnp.dot(n, wu_ref[:, c0:c0 + FF_CHUNK], preferred_element_type=F32)
            a_ref[rows, c0:c0 + FF_CHUNK] = ((g * _sigmoid(g)) * u).astype(BF16)

    def down(rows):
        f = jnp.dot(a_ref[rows, :], wd_ref[...], preferred_element_type=F32)
        h = x_ref[0, rows, :] + (0.5 * gate) * f
        if final:
            h = _rms_rows(h, fgain_ref[...])
        o_ref[0, rows, :] = h
        if emit_next:
            n_next_ref[0, :, rows] = _adaln(h, next_gain_ref[...], mod_ref, sub + 1, F32).T.astype(BF16)

    up(halves[0])
    for prev, cur in zip(halves[:-1], halves[1:]):
        up(cur)
        down(prev)
    down(halves[-1])


def _ffn(h, mod, gain, wg, wu, wd, *, sub, next_gain=None, final_gain=None):
    bsz, s, d = h.shape
    d_ff = wg.shape[1]
    tm = FFN_TOKEN_TILE
    tile = pl.BlockSpec((1, tm, d), lambda b, i: (b, i, 0))
    hbm = pl.BlockSpec(memory_space=pl.ANY)
    in_specs = [tile, pl.BlockSpec((1, N_MOD, d), lambda b, i: (b, 0, 0)), _const_spec((1, d)),
                hbm, hbm, hbm]
    args = [h, mod, gain.reshape(1, d), wg, wu, wd]
    for extra in (next_gain, final_gain):
        if extra is not None:
            in_specs.append(_const_spec((1, d)))
            args.append(extra.reshape(1, d))
    out_specs, out_shape = [tile], [jax.ShapeDtypeStruct((bsz, s, d), F32)]
    if next_gain is not None:
        out_specs.append(pl.BlockSpec((1, d, tm), lambda b, i: (b, 0, i)))
        out_shape.append(jax.ShapeDtypeStruct((bsz, d, s), BF16))
    outs = pl.pallas_call(
        functools.partial(_ffn_kernel, sub=sub, emit_next=next_gain is not None,
                          final=final_gain is not None),
        grid=(bsz, s // tm),
        in_specs=in_specs,
        out_specs=out_specs,
        out_shape=out_shape,
        scratch_shapes=[
            pltpu.VMEM((tm, d_ff), BF16),
            pltpu.VMEM((d, d_ff), BF16), pltpu.VMEM((d, d_ff), BF16), pltpu.VMEM((d_ff, d), BF16),
            pltpu.VMEM((LOAD_SLOTS, d, LOAD_CHUNK), F32), pltpu.VMEM((LOAD_SLOTS, LOAD_CHUNK, d), F32),
            pltpu.SemaphoreType.DMA((LOAD_SLOTS,)),
        ],
        compiler_params=pltpu.CompilerParams(
            dimension_semantics=("arbitrary", "arbitrary"),
            vmem_limit_bytes=V7X_VMEM_LIMIT),
        name="ffn%d" % sub,
    )(*args)
    return outs if next_gain is not None else outs[0]


def _rope_rows(x1, x2, cos, sin):
    return x1 * cos - x2 * sin, x2 * cos + x1 * sin


def _prep_kernel(*refs):
    for t0 in range(0, refs[0].shape[2], PREP_SUB_TILE):
        _prep_group(slice(t0, t0 + PREP_SUB_TILE), *refs)


def _prep_group(tok, n_ref, pos_ref, win_ref, wuq_ref, wukv_ref,
                qnorm_ref, kvnorm_ref, qgain_a_ref, kgain_a_ref, qgain_b_ref, kgain_b_ref,
                freq_a_ref, freq_b_ref,
                qa_ref, ka_ref, va_ref, qb_ref, kb_ref, vb_ref, g_ref):
    tm = PREP_SUB_TILE
    n_t = n_ref[0, :, tok]

    def proj_t(r0, r1):
        return jnp.dot(win_ref[r0:r1, :], n_t, preferred_element_type=F32)

    o_q, o_kv, o_kr = 0, MLA_Q_LORA, MLA_Q_LORA + MLA_KV_LORA
    o_qb = o_kr + MLA_ROPE
    w_b = DIFF_HEADS * DIFF_V
    o_kb, o_vb, o_g = o_qb + w_b, o_qb + 2 * w_b, o_qb + 3 * w_b
    g_chunk = g_ref.shape[1] // 4

    def emit_gates(i):
        r0 = i * g_chunk
        z = proj_t(o_g + r0, o_g + r0 + g_chunk)
        g_ref[0, r0:r0 + g_chunk, tok] = _sigmoid(z).astype(BF16)

    z_a = proj_t(o_q, o_qb)
    emit_gates(0)
    zq, zkv, kr = z_a[o_q:o_kv], z_a[o_kv:o_kr], z_a[o_kr:o_qb]
    rq = lax.rsqrt(jnp.mean(zq * zq, axis=0, keepdims=True) + NORM_EPS)
    zqn = (zq * rq * _lane_tile(qnorm_ref[...], tm)).astype(BF16)
    q_all = jnp.dot(wuq_ref[...], zqn, preferred_element_type=F32)
    rkv = lax.rsqrt(jnp.mean(zkv * zkv, axis=0, keepdims=True) + NORM_EPS)
    zkvn = (zkv * rkv * _lane_tile(kvnorm_ref[...], tm)).astype(BF16)
    kv_all = jnp.dot(wukv_ref[...], zkvn, preferred_element_type=F32)
    emit_gates(1)

    pos = pos_ref[0, :, tok].astype(F32)
    ang_a = pos * _lane_tile(freq_a_ref[...], tm)
    cos_a, sin_a = jnp.cos(ang_a), jnp.sin(ang_a)
    ang_b = pos * _lane_tile(freq_b_ref[...], tm)
    cos_b, sin_b = jnp.cos(ang_b), jnp.sin(ang_b)

    qgain_a = _lane_tile(qgain_a_ref[...], tm)
    kgain_a = _lane_tile(kgain_a_ref[...], tm)
    q_scale = LOG2E / math.sqrt(MLA_QK)
    pad_rows = jnp.zeros((HEAD_PAD - MLA_QK, tm), F32)
    sum_rows = (lax.broadcasted_iota(jnp.int32, (SUM_ROWS, tm), 0) == 0).astype(BF16)
    half = MLA_ROPE // 2

    def head_norm_rope_a(xh, gain):
        r = lax.rsqrt(jnp.sum(xh * xh, axis=0, keepdims=True) * (1.0 / MLA_QK) + NORM_EPS)
        xh = xh * r * gain
        r1, r2 = _rope_rows(xh[MLA_NOPE:MLA_NOPE + half], xh[MLA_NOPE + half:MLA_QK], cos_a, sin_a)
        return jnp.concatenate([xh[:MLA_NOPE], r1, r2, xh[MLA_QK:]], axis=0)

    zqb = proj_t(o_qb, o_kb)
    zkb = proj_t(o_kb, o_vb)
    for hd in range(MLA_HEADS):
        lo = hd * HEAD_PAD
        qh = head_norm_rope_a(q_all[lo:lo + HEAD_PAD], qgain_a) * q_scale
        qa_ref[0, lo:lo + HEAD_PAD, tok] = qh.astype(BF16)
        kvh = kv_all[lo:lo + HEAD_PAD]
        va_ref[0, hd * MLA_VS:hd * MLA_VS + MLA_V, tok] = kvh[MLA_NOPE:].astype(BF16)
        va_ref[0, hd * MLA_VS + MLA_V:(hd + 1) * MLA_VS, tok] = sum_rows
        kh = jnp.concatenate([kvh[:MLA_NOPE], kr, pad_rows], axis=0)
        kh = head_norm_rope_a(kh, kgain_a)
        ka_ref[0, tok, lo:lo + HEAD_PAD] = kh.T.astype(BF16)
    emit_gates(2)

    qgain_b = _lane_tile(qgain_b_ref[...], tm)
    kgain_b = _lane_tile(kgain_b_ref[...], tm)
    qb_scale = LOG2E / math.sqrt(DIFF_HD)
    hb = DIFF_ROT // 2

    def head_norm_rope_b(xh, gain):
        r = lax.rsqrt(jnp.mean(xh * xh, axis=0, keepdims=True) + NORM_EPS)
        xh = xh * r * gain
        r1, r2 = _rope_rows(xh[:hb], xh[hb:DIFF_ROT], cos_b, sin_b)
        return jnp.concatenate([r1, r2, xh[DIFF_ROT:]], axis=0)

    zvb = proj_t(o_vb, o_g)
    k_parts = []
    for blk in range(2 * DIFF_HEADS):
        lo = blk * DIFF_HD
        qh = head_norm_rope_b(zqb[lo:lo + DIFF_HD], qgain_b) * qb_scale
        qb_ref[0, lo:lo + DIFF_HD, tok] = qh.astype(BF16)
        k_parts.append(head_norm_rope_b(zkb[lo:lo + DIFF_HD], kgain_b))
    for hd in range(DIFF_HEADS):
        k12 = jnp.concatenate(k_parts[2 * hd:2 * hd + 2], axis=0)
        kb_ref[0, tok, hd * DIFF_V:(hd + 1) * DIFF_V] = k12.T.astype(BF16)
        vb_ref[0, hd * DIFF_VS:hd * DIFF_VS + DIFF_V, tok] = zvb[hd * DIFF_V:(hd + 1) * DIFF_V].astype(BF16)
        vb_ref[0, hd * DIFF_VS + DIFF_V:(hd + 1) * DIFF_VS, tok] = sum_rows
    emit_gates(3)


def _prep(n_t, pos3, w):
    bsz, d, s = n_t.shape
    tm = PREP_TOKEN_TILE
    n_in = w["win_t"].shape[0]
    wa, wb = MLA_HEADS * HEAD_PAD, DIFF_HEADS * DIFF_V
    n_gate = n_in - (MLA_Q_LORA + MLA_KV_LORA + MLA_ROPE + 3 * wb)

    def fm(rows):
        return (jax.ShapeDtypeStruct((bsz, rows, s), BF16),
                pl.BlockSpec((1, rows, tm), lambda b, i: (b, 0, i)))

    def tmaj(cols):
        return (jax.ShapeDtypeStruct((bsz, s, cols), BF16),
                pl.BlockSpec((1, tm, cols), lambda b, i: (b, i, 0)))

    outs = [fm(wa), tmaj(wa), fm(MLA_HEADS * MLA_VS), fm(wb), tmaj(wb), fm(DIFF_HEADS * DIFF_VS), fm(n_gate)]
    consts = [w["win_t"], w["wuq_t"], w["wukv_t"], w["qnorm"], w["kvnorm"],
              w["qgain_a"], w["kgain_a"], w["qgain_b"], w["kgain_b"], w["freq_a"], w["freq_b"]]
    in_specs = [
        pl.BlockSpec((1, d, tm), lambda b, i: (b, 0, i)),
        pl.BlockSpec((1, 1, tm), lambda b, i: (b, 0, i)),
    ] + [_const_spec(a.shape) for a in consts]
    return pl.pallas_call(
        _prep_kernel,
        grid=(bsz, s // tm),
        in_specs=in_specs,
        out_specs=[o[1] for o in outs],
        out_shape=[o[0] for o in outs],
        compiler_params=pltpu.CompilerParams(
            dimension_semantics=("arbitrary", "arbitrary"),
            vmem_limit_bytes=V7X_VMEM_LIMIT),
        name="prep",
    )(n_t, pos3, *consts)


def _normalise(acc, rows):
    return acc[:rows] * (1.0 / acc[rows:rows + 1])


def _attention_pipeline(s_len, tq, kc, qk_group, n_streams, scores_fn, v_fn, finish_fn):
    units = [(q0, c0) for q0 in range(0, s_len, tq) for c0 in range(0, s_len, kc)]
    streams = range(n_streams)
    n = len(units)
    group = [None]

    def unit_scores(q0, c0):
        g0 = c0 % (kc * qk_group)
        if g0 == 0:
            group[0] = [scores_fn(q0, c0, kc * qk_group, t) for t in streams]
        return [sg[g0:g0 + kc] for sg in group[0]]

    s_cur = unit_scores(*units[0])
    m = [None] * n_streams
    acc = [None] * n_streams
    pending = None
    for i in range(n + 1):
        s_nxt = unit_scores(*units[i + 1]) if i + 1 < n else None
        if pending is not None:
            (q0, c0), alphas, probs = pending
            v = v_fn(c0)
            for t in streams:
                o = jnp.dot(v, probs[t], preferred_element_type=F32)
                acc[t] = o if alphas[t] is None else alphas[t] * acc[t] + o
            if c0 + kc == s_len:
                finish_fn(q0, acc)
                acc = [None] * n_streams
            pending = None
        if i < n:
            q0, c0 = units[i]
            alphas, probs = [], []
            for t in streams:
                cmax = jnp.max(s_cur[t], axis=0, keepdims=True)
                if c0 == 0:
                    m[t] = cmax
                    alphas.append(None)
                else:
                    m_new = jnp.maximum(m[t], cmax)
                    alphas.append(jnp.exp2(m[t] - m_new))
                    m[t] = m_new
                probs.append(jnp.exp2(s_cur[t] - m[t]).astype(BF16))
            pending = (units[i], alphas, probs)
            s_cur = s_nxt


def _mla_kernel(k_ref, q_ref, v_ref, o_ref):
    def scores(q0, c0, rows, t):
        return jnp.dot(k_ref[0, c0:c0 + rows, :], q_ref[0, :, q0:q0 + MLA_Q_TILE],
                       preferred_element_type=F32)

    def values(c0):
        return v_ref[0, :, c0:c0 + MLA_KEY_CHUNK]

    def finish(q0, accs):
        o_ref[0, :, q0:q0 + MLA_Q_TILE] = _normalise(accs[0], MLA_V).astype(BF16)

    _attention_pipeline(k_ref.shape[1], MLA_Q_TILE, MLA_KEY_CHUNK, MLA_QK_GROUP, 1,
                        scores, values, finish)


def _mla_attention(ka, qa, va):
    bsz, s, _ = ka.shape
    return pl.pallas_call(
        _mla_kernel,
        grid=(bsz, MLA_HEADS),
        in_specs=[
            pl.BlockSpec((1, s, HEAD_PAD), lambda b, h: (b, 0, h)),
            pl.BlockSpec((1, HEAD_PAD, s), lambda b, h: (b, h, 0)),
            pl.BlockSpec((1, MLA_VS, s), lambda b, h: (b, h, 0)),
        ],
        out_specs=pl.BlockSpec((1, MLA_V, s), lambda b, h: (b, h, 0)),
        out_shape=jax.ShapeDtypeStruct((bsz, MLA_HEADS * MLA_V, s), BF16),
        compiler_params=pltpu.CompilerParams(
            dimension_semantics=("arbitrary", "arbitrary"),
            vmem_limit_bytes=V7X_VMEM_LIMIT),
        name="mla_attn",
    )(ka, qa, va)


def _diff_kernel(k_ref, q_ref, v_ref, lq1_ref, lk1_ref, lq2_ref, lk2_ref, subln_ref, o_ref, *,
                 lambda_init):
    lam = (jnp.exp(jnp.sum(lq1_ref[...] * lk1_ref[...], axis=-1, keepdims=True))
           - jnp.exp(jnp.sum(lq2_ref[...] * lk2_ref[...], axis=-1, keepdims=True))
           + lambda_init)
    subln = _lane_tile(subln_ref[...], DIFF_Q_TILE)
    zeros = jnp.zeros((DIFF_HD, DIFF_Q_TILE), BF16)

    def scores(q0, c0, rows, t):
        q12 = q_ref[0, :, q0:q0 + DIFF_Q_TILE]
        k12 = k_ref[0, c0:c0 + rows, :]
        if t == 0:
            q = jnp.concatenate([q12[:DIFF_HD], zeros], axis=0)
        else:
            q = jnp.concatenate([zeros, q12[DIFF_HD:]], axis=0)
        return jnp.dot(k12, q, preferred_element_type=F32)

    def values(c0):
        return v_ref[0, :, c0:c0 + DIFF_KEY_CHUNK]

    def finish(q0, accs):
        o = _normalise(accs[0], DIFF_V) - lam * _normalise(accs[1], DIFF_V)
        r = lax.rsqrt(jnp.mean(o * o, axis=0, keepdims=True) + NORM_EPS)
        o_ref[0, :, q0:q0 + DIFF_Q_TILE] = ((o * r * subln) * (1.0 - lambda_init)).astype(BF16)

    _attention_pipeline(k_ref.shape[1], DIFF_Q_TILE, DIFF_KEY_CHUNK, DIFF_QK_GROUP, 2,
                        scores, values, finish)


def _diff_attention(kb, qb, vb, lq1, lk1, lq2, lk2, subln, lambda_init):
    bsz, s, _ = kb.shape
    vec = _const_spec((1, DIFF_HD))
    return pl.pallas_call(
        functools.partial(_diff_kernel, lambda_init=lambda_init),
        grid=(bsz, DIFF_HEADS),
        in_specs=[
            pl.BlockSpec((1, s, DIFF_V), lambda b, h: (b, 0, h)),
            pl.BlockSpec((1, DIFF_V, s), lambda b, h: (b, h, 0)),
            pl.BlockSpec((1, DIFF_VS, s), lambda b, h: (b, h, 0)),
            vec, vec, vec, vec,
            _const_spec((DIFF_V, LANES)),
        ],
        out_specs=pl.BlockSpec((1, DIFF_V, s), lambda b, h: (b, h, 0)),
        out_shape=jax.ShapeDtypeStruct((bsz, DIFF_HEADS * DIFF_V, s), BF16),
        compiler_params=pltpu.CompilerParams(
            dimension_semantics=("arbitrary", "arbitrary"),
            vmem_limit_bytes=V7X_VMEM_LIMIT),
        name="diff_attn",
    )(kb, qb, vb, lq1, lk1, lq2, lk2, subln)


def _merge_kernel(oa_ref, ob_ref, g_ref, h_ref, mod_ref, woa_ref, wob_ref, wout_ref, o_ref):
    d = h_ref.shape[2]
    gate = mod_ref[0, 5:6, :]
    groups = [slice(t0, t0 + MERGE_SUB_TILE) for t0 in range(0, h_ref.shape[1], MERGE_SUB_TILE)]

    def branch_mix(tok):
        ya = jnp.dot(woa_ref[...], oa_ref[0, :, tok], preferred_element_type=F32)
        yb = jnp.dot(wob_ref[...], ob_ref[0, :, tok], preferred_element_type=F32)
        ga = g_ref[0, :d, tok].astype(F32)
        gb = g_ref[0, d:, tok].astype(F32)
        return (ga * ya + gb * yb).astype(BF16)

    def project(tok, mix):
        y_t = jnp.dot(wout_ref[...], mix, preferred_element_type=F32)
        o_ref[0, tok, :] = h_ref[0, tok, :] + gate * y_t.T

    mixes = [branch_mix(groups[0])]
    for prev, cur in zip(groups[:-1], groups[1:]):
        mixes.append(branch_mix(cur))
        project(prev, mixes[-2])
    project(groups[-1], mixes[-1])


def _merge(oa, ob, g, h, mod, woa_t, wob_t, wout_t):
    bsz, s, d = h.shape
    tm = MERGE_TOKEN_TILE
    return pl.pallas_call(
        _merge_kernel,
        grid=(bsz, s // tm),
        in_specs=[
            pl.BlockSpec((1, oa.shape[1], tm), lambda b, i: (b, 0, i)),
            pl.BlockSpec((1, ob.shape[1], tm), lambda b, i: (b, 0, i)),
            pl.BlockSpec((1, g.shape[1], tm), lambda b, i: (b, 0, i)),
            pl.BlockSpec((1, tm, d), lambda b, i: (b, i, 0)),
            pl.BlockSpec((1, N_MOD, d), lambda b, i: (b, 0, 0)),
            _const_spec(woa_t.shape),
            _const_spec(wob_t.shape),
            _const_spec(wout_t.shape),
        ],
        out_specs=pl.BlockSpec((1, tm, d), lambda b, i: (b, i, 0)),
        out_shape=jax.ShapeDtypeStruct((bsz, s, d), F32),
        compiler_params=pltpu.CompilerParams(
            dimension_semantics=("arbitrary", "arbitrary"),
            vmem_limit_bytes=V7X_VMEM_LIMIT),
        name="merge",
    )(oa, ob, g, h, mod, woa_t, wob_t, wout_t)


def _lane_bcast(v, rows=None):
    n = v.shape[0]
    out = jnp.broadcast_to(v.astype(F32)[:, None], (n, LANES))
    if rows is not None and rows > n:
        out = jnp.pad(out, ((0, rows - n), (0, 0)))
    return out


def _rope_freqs(half, theta):
    return 1.0 / (theta ** (jnp.arange(half, dtype=F32) / half))


def _layer_weights(l, w_in, mla_q_norm, mla_w_uq, mla_kv_norm, mla_w_ukv, mla_q_gain, mla_k_gain,
                   diff_q_gain, diff_k_gain):
    wuq = mla_w_uq[l].reshape(MLA_Q_LORA, MLA_HEADS, MLA_QK)
    wuq = jnp.pad(wuq, ((0, 0), (0, 0), (0, HEAD_PAD - MLA_QK))).reshape(MLA_Q_LORA, -1)
    return {
        "win_t": w_in[l].T.astype(BF16),
        "wuq_t": wuq.T.astype(BF16),
        "wukv_t": mla_w_ukv[l].T.astype(BF16),
        "qnorm": _lane_bcast(mla_q_norm[l]),
        "kvnorm": _lane_bcast(mla_kv_norm[l]),
        "qgain_a": _lane_bcast(mla_q_gain[l], HEAD_PAD),
        "kgain_a": _lane_bcast(mla_k_gain[l], HEAD_PAD),
        "qgain_b": _lane_bcast(diff_q_gain[l]),
        "kgain_b": _lane_bcast(diff_k_gain[l]),
        "freq_a": _lane_bcast(_rope_freqs(MLA_ROPE // 2, MLA_THETA)),
        "freq_b": _lane_bcast(_rope_freqs(DIFF_ROT // 2, DIFF_THETA)),
    }


def kernel(x, c, positions, w_ada, b_ada, ffn1_norm, ffn1_w_gate, ffn1_w_up, ffn1_w_down, mix_norm, w_in, mla_q_norm, mla_w_uq, mla_kv_norm, mla_w_ukv, mla_q_gain, mla_k_gain, mla_w_o, diff_q_gain, diff_k_gain, diff_lambda_q1, diff_lambda_k1, diff_lambda_q2, diff_lambda_k2, diff_subln, diff_w_o, w_out, ffn2_norm, ffn2_w_gate, ffn2_w_up, ffn2_w_down, final_norm):
    bsz, s, d = x.shape
    depth = w_ada.shape[0]
    pos3 = positions.reshape(bsz, 1, s)
    h = x
    for l in range(depth):
        lambda_init = 0.8 - 0.6 * math.exp(-0.3 * l)
        mod = _ada(c, w_ada[l], b_ada[l]).reshape(bsz, N_MOD, d)

        h, n_mix = _ffn(h, mod, ffn1_norm[l], ffn1_w_gate[l], ffn1_w_up[l], ffn1_w_down[l],
                        sub=0, next_gain=mix_norm[l])

        w = _layer_weights(l, w_in, mla_q_norm, mla_w_uq, mla_kv_norm, mla_w_ukv, mla_q_gain,
                           mla_k_gain, diff_q_gain, diff_k_gain)
        qa, ka, va, qb, kb, vb, g = _prep(n_mix, pos3, w)
        oa = _mla_attention(ka, qa, va)
        ob = _diff_attention(kb, qb, vb,
                             diff_lambda_q1[l].reshape(1, -1), diff_lambda_k1[l].reshape(1, -1),
                             diff_lambda_q2[l].reshape(1, -1), diff_lambda_k2[l].reshape(1, -1),
                             _lane_bcast(diff_subln[l]), lambda_init)
        h = _merge(oa, ob, g, h, mod, mla_w_o[l].T.astype(BF16), diff_w_o[l].T.astype(BF16),
                   w_out[l].T.astype(BF16))

        h = _ffn(h, mod, ffn2_norm[l], ffn2_w_gate[l], ffn2_w_up[l], ffn2_w_down[l],
                 sub=2, final_gain=final_norm[l])
    return h
```

```python
import functools
import math

import jax
import jax.numpy as jnp
from jax import lax
from jax.experimental import pallas as pl
from jax.experimental.pallas import tpu as pltpu

F32 = jnp.float32
BF16 = jnp.bfloat16

NORM_EPS = 1e-6
N_MOD = 9

MLA_HEADS = 8
MLA_NOPE = 64
MLA_ROPE = 32
MLA_QK = MLA_NOPE + MLA_ROPE
MLA_V = 64
MLA_Q_LORA = 384
MLA_KV_LORA = 256
MLA_THETA = 10000.0
DIFF_HEADS = 4
DIFF_HD = 64
DIFF_V = 2 * DIFF_HD
DIFF_THETA = 500000.0
DIFF_ROT = DIFF_HD // 4

LANES = 128
HEAD_PAD = 128
SUM_ROWS = 16
MLA_VS = MLA_V + SUM_ROWS
DIFF_VS = DIFF_V + SUM_ROWS
LOG2E = math.log2(math.e)
V7X_VMEM_LIMIT = 56 * 1024 * 1024

PREP_TOKEN_TILE = 1024
PREP_SUB_TILE = 512
FFN_TOKEN_TILE = 1024
FFN_SUB_TILE = 512
MERGE_TOKEN_TILE = 1024
MERGE_SUB_TILE = 512
MLA_Q_TILE = 512
DIFF_Q_TILE = 512
MLA_KEY_CHUNK = 256
DIFF_KEY_CHUNK = 512
MLA_QK_GROUP = 1
DIFF_QK_GROUP = 1
FF_CHUNK = 256
LOAD_CHUNK = 128
LOAD_SLOTS = 4
ADA_COL_BLOCK = 1024


def _sigmoid(x):
    return 1.0 / (1.0 + jnp.exp(-x))


def _rms_rows(x, gain):
    ms = jnp.mean(x * x, axis=-1, keepdims=True)
    return x * lax.rsqrt(ms + NORM_EPS) * gain


def _lane_tile(g, width):
    return jnp.tile(g, (1, width // LANES))


def _const_spec(shape):
    return pl.BlockSpec(shape, lambda *_: (0,) * len(shape), pipeline_mode=pl.Buffered(1))


def _ada_kernel(c_ref, w_ref, b_ref, o_ref):
    c = c_ref[...]
    bsz = c.shape[0]
    cond = c * _sigmoid(c)
    c_hi = cond.astype(BF16).astype(F32)
    lhs = jnp.concatenate([c_hi, cond - c_hi], axis=0).astype(BF16)
    w = w_ref[...]
    w_hi = w.astype(BF16)
    w_lo = (w - w_hi.astype(F32)).astype(BF16)
    a = jnp.dot(lhs, w_hi, preferred_element_type=F32)
    b = jnp.dot(lhs, w_lo, preferred_element_type=F32)
    o_ref[...] = a[:bsz] + a[bsz:] + b[:bsz] + b_ref[...]


def _ada(c, w_ada, b_ada):
    bsz, d = c.shape
    cols = w_ada.shape[1]
    return pl.pallas_call(
        _ada_kernel,
        grid=(cols // ADA_COL_BLOCK,),
        in_specs=[
            pl.BlockSpec((bsz, d), lambda j: (0, 0)),
            pl.BlockSpec((d, ADA_COL_BLOCK), lambda j: (0, j)),
            pl.BlockSpec((1, ADA_COL_BLOCK), lambda j: (0, j)),
        ],
        out_specs=pl.BlockSpec((bsz, ADA_COL_BLOCK), lambda j: (0, j)),
        out_shape=jax.ShapeDtypeStruct((bsz, cols), F32),
        compiler_params=pltpu.CompilerParams(dimension_semantics=("arbitrary",)),
        name="ada",
    )(c, w_ada, b_ada.reshape(1, cols))


def _adaln(x, gain, mod_ref, sub, dtype=BF16):
    shift = mod_ref[0, 3 * sub:3 * sub + 1, :]
    scale = mod_ref[0, 3 * sub + 1:3 * sub + 2, :]
    return (_rms_rows(x, gain) * (1.0 + scale) + shift).astype(dtype)


def _load_ffn_weights(wg_hbm, wu_hbm, wd_hbm, wg_s, wu_s, wd_s, stage_c, stage_r, sem):
    jobs = []
    for r0 in range(0, wg_s.shape[0], LOAD_CHUNK):
        rows = slice(r0, r0 + LOAD_CHUNK)
        jobs.append((wg_hbm.at[rows, :], stage_c, wg_s.at[rows, :]))
        jobs.append((wu_hbm.at[rows, :], stage_c, wu_s.at[rows, :]))
    for r0 in range(0, wd_s.shape[0], LOAD_CHUNK):
        rows = slice(r0, r0 + LOAD_CHUNK)
        jobs.append((wd_hbm.at[rows, :], stage_r, wd_s.at[rows, :]))

    def copy(i):
        src, stage, _ = jobs[i]
        return pltpu.make_async_copy(src, stage.at[i % LOAD_SLOTS], sem.at[i % LOAD_SLOTS])

    ahead = LOAD_SLOTS - 1
    for i in range(min(ahead, len(jobs))):
        copy(i).start()
    for i, (_, stage, dst) in enumerate(jobs):
        if i + ahead < len(jobs):
            copy(i + ahead).start()
        copy(i).wait()
        dst[...] = stage[i % LOAD_SLOTS].astype(BF16)


def _ffn_kernel(*refs, sub, emit_next, final):
    refs = list(refs)
    x_ref, mod_ref, gain_ref = refs.pop(0), refs.pop(0), refs.pop(0)
    w_hbm = [refs.pop(0) for _ in range(3)]
    next_gain_ref = refs.pop(0) if emit_next else None
    fgain_ref = refs.pop(0) if final else None
    o_ref = refs.pop(0)
    n_next_ref = refs.pop(0) if emit_next else None
    a_ref, wg_ref, wu_ref, wd_ref, stage_c, stage_r, sem = refs

    @pl.when((pl.program_id(0) == 0) & (pl.program_id(1) == 0))
    def _():
        _load_ffn_weights(*w_hbm, wg_ref, wu_ref, wd_ref, stage_c, stage_r, sem)

    d_ff = wg_ref.shape[1]
    gate = mod_ref[0, 3 * sub + 2:3 * sub + 3, :]
    halves = [slice(r0, r0 + FFN_SUB_TILE) for r0 in range(0, x_ref.shape[1], FFN_SUB_TILE)]

    def up(rows):
        n = _adaln(x_ref[0, rows, :], gain_ref[...], mod_ref, sub)
        for c0 in range(0, d_ff, FF_CHUNK):
            g = jnp.dot(n, wg_ref[:, c0:c0 + FF_CHUNK], preferred_element_type=F32)
            u = jnp.dot(n, wu_ref[:, c0:c0 + FF_CHUNK], preferred_element_type=F32)
            a_ref[rows, c0:c0 + FF_CHUNK] = ((g * _sigmoid(g)) * u).astype(BF16)

    def down(rows):
        f = jnp.dot(a_ref[rows, :], wd_ref[...], preferred_element_type=F32)
        h = x_ref[0, rows, :] + (0.5 * gate) * f
        if final:
            h = _rms_rows(h, fgain_ref[...])
        o_ref[0, rows, :] = h
        if emit_next:
            n_next_ref[0, :, rows] = _adaln(h, next_gain_ref[...], mod_ref, sub + 1, F32).T.astype(BF16)

    up(halves[0])
    for prev, cur in zip(halves[:-1], halves[1:]):
        up(cur)
        down(prev)
    down(halves[-1])


def _ffn(h, mod, gain, wg, wu, wd, *, sub, next_gain=None, final_gain=None):
    bsz, s, d = h.shape
    d_ff = wg.shape[1]
    tm = FFN_TOKEN_TILE
    tile = pl.BlockSpec((1, tm, d), lambda b, i: (b, i, 0))
    hbm = pl.BlockSpec(memory_space=pl.ANY)
    in_specs = [tile, pl.BlockSpec((1, N_MOD, d), lambda b, i: (b, 0, 0)), _const_spec((1, d)),
                hbm, hbm, hbm]
    args = [h, mod, gain.reshape(1, d), wg, wu, wd]
    for extra in (next_gain, final_gain):
        if extra is not None:
            in_specs.append(_const_spec((1, d)))
            args.append(extra.reshape(1, d))
    out_specs, out_shape = [tile], [jax.ShapeDtypeStruct((bsz, s, d), F32)]
    if next_gain is not None:
        out_specs.append(pl.BlockSpec((1, d, tm), lambda b, i: (b, 0, i)))
        out_shape.append(jax.ShapeDtypeStruct((bsz, d, s), BF16))
    outs = pl.pallas_call(
        functools.partial(_ffn_kernel, sub=sub, emit_next=next_gain is not None,
                          final=final_gain is not None),
        grid=(bsz, s // tm),
        in_specs=in_specs,
        out_specs=out_specs,
        out_shape=out_shape,
        scratch_shapes=[
            pltpu.VMEM((tm, d_ff), BF16),
            pltpu.VMEM((d, d_ff), BF16), pltpu.VMEM((d, d_ff), BF16), pltpu.VMEM((d_ff, d), BF16),
            pltpu.VMEM((LOAD_SLOTS, LOAD_CHUNK, d_ff), F32), pltpu.VMEM((LOAD_SLOTS, LOAD_CHUNK, d), F32),
            pltpu.SemaphoreType.DMA((LOAD_SLOTS,)),
        ],
        compiler_params=pltpu.CompilerParams(
            dimension_semantics=("arbitrary", "arbitrary"),
            vmem_limit_bytes=V7X_VMEM_LIMIT),
        name="ffn%d" % sub,
    )(*args)
    return outs if next_gain is not None else outs[0]


def _rope_rows(x1, x2, cos, sin):
    return x1 * cos - x2 * sin, x2 * cos + x1 * sin


def _prep_kernel(*refs):
    for t0 in range(0, refs[0].shape[2], PREP_SUB_TILE):
        _prep_group(slice(t0, t0 + PREP_SUB_TILE), *refs)


def _prep_group(tok, n_ref, pos_ref, win_ref, wuq_ref, wukv_ref,
                qnorm_ref, kvnorm_ref, qgain_a_ref, kgain_a_ref, qgain_b_ref, kgain_b_ref,
                freq_a_ref, freq_b_ref,
                qa_ref, ka_ref, va_ref, qb_ref, kb_ref, vb_ref, g_ref):
    tm = PREP_SUB_TILE
    n_t = n_ref[0, :, tok]

    def proj_t(r0, r1):
        return jnp.dot(win_ref[r0:r1, :], n_t, preferred_element_type=F32)

    o_q, o_kv, o_kr = 0, MLA_Q_LORA, MLA_Q_LORA + MLA_KV_LORA
    o_qb = o_kr + MLA_ROPE
    w_b = DIFF_HEADS * DIFF_V
    o_kb, o_vb, o_g = o_qb + w_b, o_qb + 2 * w_b, o_qb + 3 * w_b
    g_chunk = g_ref.shape[1] // 4

    def emit_gates(i):
        r0 = i * g_chunk
        z = proj_t(o_g + r0, o_g + r0 + g_chunk)
        g_ref[0, r0:r0 + g_chunk, tok] = _sigmoid(z).astype(BF16)

    z_a = proj_t(o_q, o_qb)
    emit_gates(0)
    zq, zkv, kr = z_a[o_q:o_kv], z_a[o_kv:o_kr], z_a[o_kr:o_qb]
    rq = lax.rsqrt(jnp.mean(zq * zq, axis=0, keepdims=True) + NORM_EPS)
    zqn = (zq * rq * _lane_tile(qnorm_ref[...], tm)).astype(BF16)
    q_all = jnp.dot(wuq_ref[...], zqn, preferred_element_type=F32)
    rkv = lax.rsqrt(jnp.mean(zkv * zkv, axis=0, keepdims=True) + NORM_EPS)
    zkvn = (zkv * rkv * _lane_tile(kvnorm_ref[...], tm)).astype(BF16)
    kv_all = jnp.dot(wukv_ref[...], zkvn, preferred_element_type=F32)
    emit_gates(1)

    pos = pos_ref[0, :, tok].astype(F32)
    ang_a = pos * _lane_tile(freq_a_ref[...], tm)
    cos_a, sin_a = jnp.cos(ang_a), jnp.sin(ang_a)
    ang_b = pos * _lane_tile(freq_b_ref[...], tm)
    cos_b, sin_b = jnp.cos(ang_b), jnp.sin(ang_b)

    qgain_a = _lane_tile(qgain_a_ref[...], tm)
    kgain_a = _lane_tile(kgain_a_ref[...], tm)
    q_scale = LOG2E / math.sqrt(MLA_QK)
    pad_rows = jnp.zeros((HEAD_PAD - MLA_QK, tm), F32)
    sum_rows = (lax.broadcasted_iota(jnp.int32, (SUM_ROWS, tm), 0) == 0).astype(BF16)
    half = MLA_ROPE // 2

    def head_norm_rope_a(xh, gain):
        r = lax.rsqrt(jnp.sum(xh * xh, axis=0, keepdims=True) * (1.0 / MLA_QK) + NORM_EPS)
        xh = xh * r * gain
        r1, r2 = _rope_rows(xh[MLA_NOPE:MLA_NOPE + half], xh[MLA_NOPE + half:MLA_QK], cos_a, sin_a)
        return jnp.concatenate([xh[:MLA_NOPE], r1, r2, xh[MLA_QK:]], axis=0)

    zqb = proj_t(o_qb, o_kb)
    zkb = proj_t(o_kb, o_vb)
    for hd in range(MLA_HEADS):
        lo = hd * HEAD_PAD
        qh = head_norm_rope_a(q_all[lo:lo + HEAD_PAD], qgain_a) * q_scale
        qa_ref[0, lo:lo + HEAD_PAD, tok] = qh.astype(BF16)
        kvh = kv_all[lo:lo + HEAD_PAD]
        va_ref[0, hd * MLA_VS:hd * MLA_VS + MLA_V, tok] = kvh[MLA_NOPE:].astype(BF16)
        va_ref[0, hd * MLA_VS + MLA_V:(hd + 1) * MLA_VS, tok] = sum_rows
        kh = jnp.concatenate([kvh[:MLA_NOPE], kr, pad_rows], axis=0)
        kh = head_norm_rope_a(kh, kgain_a)
        ka_ref[0, tok, lo:lo + HEAD_PAD] = kh.T.astype(BF16)
    emit_gates(2)

    qgain_b = _lane_tile(qgain_b_ref[...], tm)
    kgain_b = _lane_tile(kgain_b_ref[...], tm)
    qb_scale = LOG2E / math.sqrt(DIFF_HD)
    hb = DIFF_ROT // 2

    def head_norm_rope_b(xh, gain):
        r = lax.rsqrt(jnp.mean(xh * xh, axis=0, keepdims=True) + NORM_EPS)
        xh = xh * r * gain
        r1, r2 = _rope_rows(xh[:hb], xh[hb:DIFF_ROT], cos_b, sin_b)
        return jnp.concatenate([r1, r2, xh[DIFF_ROT:]], axis=0)

    zvb = proj_t(o_vb, o_g)
    k_parts = []
    for blk in range(2 * DIFF_HEADS):
        lo = blk * DIFF_HD
        qh = head_norm_rope_b(zqb[lo:lo + DIFF_HD], qgain_b) * qb_scale
        qb_ref[0, lo:lo + DIFF_HD, tok] = qh.astype(BF16)
        k_parts.append(head_norm_rope_b(zkb[lo:lo + DIFF_HD], kgain_b))
    for hd in range(DIFF_HEADS):
        k12 = jnp.concatenate(k_parts[2 * hd:2 * hd + 2], axis=0)
        kb_ref[0, tok, hd * DIFF_V:(hd + 1) * DIFF_V] = k12.T.astype(BF16)
        vb_ref[0, hd * DIFF_VS:hd * DIFF_VS + DIFF_V, tok] = zvb[hd * DIFF_V:(hd + 1) * DIFF_V].astype(BF16)
        vb_ref[0, hd * DIFF_VS + DIFF_V:(hd + 1) * DIFF_VS, tok] = sum_rows
    emit_gates(3)


def _prep(n_t, pos3, w):
    bsz, d, s = n_t.shape
    tm = PREP_TOKEN_TILE
    n_in = w["win_t"].shape[0]
    wa, wb = MLA_HEADS * HEAD_PAD, DIFF_HEADS * DIFF_V
    n_gate = n_in - (MLA_Q_LORA + MLA_KV_LORA + MLA_ROPE + 3 * wb)

    def fm(rows):
        return (jax.ShapeDtypeStruct((bsz, rows, s), BF16),
                pl.BlockSpec((1, rows, tm), lambda b, i: (b, 0, i)))

    def tmaj(cols):
        return (jax.ShapeDtypeStruct((bsz, s, cols), BF16),
                pl.BlockSpec((1, tm, cols), lambda b, i: (b, i, 0)))

    outs = [fm(wa), tmaj(wa), fm(MLA_HEADS * MLA_VS), fm(wb), tmaj(wb), fm(DIFF_HEADS * DIFF_VS), fm(n_gate)]
    consts = [w["win_t"], w["wuq_t"], w["wukv_t"], w["qnorm"], w["kvnorm"],
              w["qgain_a"], w["kgain_a"], w["qgain_b"], w["kgain_b"], w["freq_a"], w["freq_b"]]
    in_specs = [
        pl.BlockSpec((1, d, tm), lambda b, i: (b, 0, i)),
        pl.BlockSpec((1, 1, tm), lambda b, i: (b, 0, i)),
    ] + [_const_spec(a.shape) for a in consts]
    return pl.pallas_call(
        _prep_kernel,
        grid=(bsz, s // tm),
        in_specs=in_specs,
        out_specs=[o[1] for o in outs],
        out_shape=[o[0] for o in outs],
        compiler_params=pltpu.CompilerParams(
            dimension_semantics=("arbitrary", "arbitrary"),
            vmem_limit_bytes=V7X_VMEM_LIMIT),
        name="prep",
    )(n_t, pos3, *consts)


def _normalise(acc, rows):
    return acc[:rows] * (1.0 / acc[rows:rows + 1])


def _attention_pipeline(s_len, tq, kc, qk_group, n_streams, scores_fn, v_fn, finish_fn):
    units = [(q0, c0) for q0 in range(0, s_len, tq) for c0 in range(0, s_len, kc)]
    streams = range(n_streams)
    n = len(units)
    group = [None]

    def unit_scores(q0, c0):
        g0 = c0 % (kc * qk_group)
        if g0 == 0:
            group[0] = [scores_fn(q0, c0, kc * qk_group, t) for t in streams]
        return [sg[g0:g0 + kc] for sg in group[0]]

    s_cur = unit_scores(*units[0])
    m = [None] * n_streams
    acc = [None] * n_streams
    pending = None
    for i in range(n + 1):
        s_nxt = unit_scores(*units[i + 1]) if i + 1 < n else None
        if pending is not None:
            (q0, c0), alphas, probs = pending
            v = v_fn(c0)
            for t in streams:
                o = jnp.dot(v, probs[t], preferred_element_type=F32)
                acc[t] = o if alphas[t] is None else alphas[t] * acc[t] + o
            if c0 + kc == s_len:
                finish_fn(q0, acc)
                acc = [None] * n_streams
            pending = None
        if i < n:
            q0, c0 = units[i]
            alphas, probs = [], []
            for t in streams:
                cmax = jnp.max(s_cur[t], axis=0, keepdims=True)
                if c0 == 0:
                    m[t] = cmax
                    alphas.append(None)
                else:
                    m_new = jnp.maximum(m[t], cmax)
                    alphas.append(jnp.exp2(m[t] - m_new))
                    m[t] = m_new
                probs.append(jnp.exp2(s_cur[t] - m[t]).astype(BF16))
            pending = (units[i], alphas, probs)
            s_cur = s_nxt


def _mla_kernel(k_ref, q_ref, v_ref, o_ref):
    def scores(q0, c0, rows, t):
        return jnp.dot(k_ref[0, c0:c0 + rows, :], q_ref[0, :, q0:q0 + MLA_Q_TILE],
                       preferred_element_type=F32)

    def values(c0):
        return v_ref[0, :, c0:c0 + MLA_KEY_CHUNK]

    def finish(q0, accs):
        o_ref[0, :, q0:q0 + MLA_Q_TILE] = _normalise(accs[0], MLA_V).astype(BF16)

    _attention_pipeline(k_ref.shape[1], MLA_Q_TILE, MLA_KEY_CHUNK, MLA_QK_GROUP, 1,
                        scores, values, finish)


def _mla_attention(ka, qa, va):
    bsz, s, _ = ka.shape
    return pl.pallas_call(
        _mla_kernel,
        grid=(bsz, MLA_HEADS),
        in_specs=[
            pl.BlockSpec((1, s, HEAD_PAD), lambda b, h: (b, 0, h)),
            pl.BlockSpec((1, HEAD_PAD, s), lambda b, h: (b, h, 0)),
            pl.BlockSpec((1, MLA_VS, s), lambda b, h: (b, h, 0)),
        ],
        out_specs=pl.BlockSpec((1, MLA_V, s), lambda b, h: (b, h, 0)),
        out_shape=jax.ShapeDtypeStruct((bsz, MLA_HEADS * MLA_V, s), BF16),
        compiler_params=pltpu.CompilerParams(
            dimension_semantics=("arbitrary", "arbitrary"),
            vmem_limit_bytes=V7X_VMEM_LIMIT),
        name="mla_attn",
    )(ka, qa, va)


def _diff_kernel(k_ref, q_ref, v_ref, lq1_ref, lk1_ref, lq2_ref, lk2_ref, subln_ref, o_ref, *,
                 lambda_init):
    lam = (jnp.exp(jnp.sum(lq1_ref[...] * lk1_ref[...], axis=-1, keepdims=True))
           - jnp.exp(jnp.sum(lq2_ref[...] * lk2_ref[...], axis=-1, keepdims=True))
           + lambda_init)
    subln = _lane_tile(subln_ref[...], DIFF_Q_TILE)
    zeros = jnp.zeros((DIFF_HD, DIFF_Q_TILE), BF16)

    def scores(q0, c0, rows, t):
        q12 = q_ref[0, :, q0:q0 + DIFF_Q_TILE]
        k12 = k_ref[0, c0:c0 + rows, :]
        if t == 0:
            q = jnp.concatenate([q12[:DIFF_HD], zeros], axis=0)
        else:
            q = jnp.concatenate([zeros, q12[DIFF_HD:]], axis=0)
        return jnp.dot(k12, q, preferred_element_type=F32)

    def values(c0):
        return v_ref[0, :, c0:c0 + DIFF_KEY_CHUNK]

    def finish(q0, accs):
        o = _normalise(accs[0], DIFF_V) - lam * _normalise(accs[1], DIFF_V)
        r = lax.rsqrt(jnp.mean(o * o, axis=0, keepdims=True) + NORM_EPS)
        o_ref[0, :, q0:q0 + DIFF_Q_TILE] = ((o * r * subln) * (1.0 - lambda_init)).astype(BF16)

    _attention_pipeline(k_ref.shape[1], DIFF_Q_TILE, DIFF_KEY_CHUNK, DIFF_QK_GROUP, 2,
                        scores, values, finish)


def _diff_attention(kb, qb, vb, lq1, lk1, lq2, lk2, subln, lambda_init):
    bsz, s, _ = kb.shape
    vec = _const_spec((1, DIFF_HD))
    return pl.pallas_call(
        functools.partial(_diff_kernel, lambda_init=lambda_init),
        grid=(bsz, DIFF_HEADS),
        in_specs=[
            pl.BlockSpec((1, s, DIFF_V), lambda b, h: (b, 0, h)),
            pl.BlockSpec((1, DIFF_V, s), lambda b, h: (b, h, 0)),
            pl.BlockSpec((1, DIFF_VS, s), lambda b, h: (b, h, 0)),
            vec, vec, vec, vec,
            _const_spec((DIFF_V, LANES)),
        ],
        out_specs=pl.BlockSpec((1, DIFF_V, s), lambda b, h: (b, h, 0)),
        out_shape=jax.ShapeDtypeStruct((bsz, DIFF_HEADS * DIFF_V, s), BF16),
        compiler_params=pltpu.CompilerParams(
            dimension_semantics=("arbitrary", "arbitrary"),
            vmem_limit_bytes=V7X_VMEM_LIMIT),
        name="diff_attn",
    )(kb, qb, vb, lq1, lk1, lq2, lk2, subln)


def _merge_kernel(oa_ref, ob_ref, g_ref, h_ref, mod_ref, woa_ref, wob_ref, wout_ref, o_ref):
    d = h_ref.shape[2]
    gate = mod_ref[0, 5:6, :]
    groups = [slice(t0, t0 + MERGE_SUB_TILE) for t0 in range(0, h_ref.shape[1], MERGE_SUB_TILE)]

    def branch_mix(tok):
        ya = jnp.dot(woa_ref[...], oa_ref[0, :, tok], preferred_element_type=F32)
        yb = jnp.dot(wob_ref[...], ob_ref[0, :, tok], preferred_element_type=F32)
        ga = g_ref[0, :d, tok].astype(F32)
        gb = g_ref[0, d:, tok].astype(F32)
        return (ga * ya + gb * yb).astype(BF16)

    def project(tok, mix):
        y_t = jnp.dot(wout_ref[...], mix, preferred_element_type=F32)
        o_ref[0, tok, :] = h_ref[0, tok, :] + gate * y_t.T

    mixes = [branch_mix(groups[0])]
    for prev, cur in zip(groups[:-1], groups[1:]):
        mixes.append(branch_mix(cur))
        project(prev, mixes[-2])
    project(groups[-1], mixes[-1])


def _merge(oa, ob, g, h, mod, woa_t, wob_t, wout_t):
    bsz, s, d = h.shape
    tm = MERGE_TOKEN_TILE
    return pl.pallas_call(
        _merge_kernel,
        grid=(bsz, s // tm),
        in_specs=[
            pl.BlockSpec((1, oa.shape[1], tm), lambda b, i: (b, 0, i)),
            pl.BlockSpec((1, ob.shape[1], tm), lambda b, i: (b, 0, i)),
            pl.BlockSpec((1, g.shape[1], tm), lambda b, i: (b, 0, i)),
            pl.BlockSpec((1, tm, d), lambda b, i: (b, i, 0)),
            pl.BlockSpec((1, N_MOD, d), lambda b, i: (b, 0, 0)),
            _const_spec(woa_t.shape),
            _const_spec(wob_t.shape),
            _const_spec(wout_t.shape),
        ],
        out_specs=pl.BlockSpec((1, tm, d), lambda b, i: (b, i, 0)),
        out_shape=jax.ShapeDtypeStruct((bsz, s, d), F32),
        compiler_params=pltpu.CompilerParams(
            dimension_semantics=("arbitrary", "arbitrary"),
            vmem_limit_bytes=V7X_VMEM_LIMIT),
        name="merge",
    )(oa, ob, g, h, mod, woa_t, wob_t, wout_t)


def _lane_bcast(v, rows=None):
    n = v.shape[0]
    out = jnp.broadcast_to(v.astype(F32)[:, None], (n, LANES))
    if rows is not None and rows > n:
        out = jnp.pad(out, ((0, rows - n), (0, 0)))
    return out


def _rope_freqs(half, theta):
    return 1.0 / (theta ** (jnp.arange(half, dtype=F32) / half))


def _layer_weights(l, w_in, mla_q_norm, mla_w_uq, mla_kv_norm, mla_w_ukv, mla_q_gain, mla_k_gain,
                   diff_q_gain, diff_k_gain):
    wuq = mla_w_uq[l].reshape(MLA_Q_LORA, MLA_HEADS, MLA_QK)
    wuq = jnp.pad(wuq, ((0, 0), (0, 0), (0, HEAD_PAD - MLA_QK))).reshape(MLA_Q_LORA, -1)
    return {
        "win_t": w_in[l].T.astype(BF16),
        "wuq_t": wuq.T.astype(BF16),
        "wukv_t": mla_w_ukv[l].T.astype(BF16),
        "qnorm": _lane_bcast(mla_q_norm[l]),
        "kvnorm": _lane_bcast(mla_kv_norm[l]),
        "qgain_a": _lane_bcast(mla_q_gain[l], HEAD_PAD),
        "kgain_a": _lane_bcast(mla_k_gain[l], HEAD_PAD),
        "qgain_b": _lane_bcast(diff_q_gain[l]),
        "kgain_b": _lane_bcast(diff_k_gain[l]),
        "freq_a": _lane_bcast(_rope_freqs(MLA_ROPE // 2, MLA_THETA)),
        "freq_b": _lane_bcast(_rope_freqs(DIFF_ROT // 2, DIFF_THETA)),
    }


def kernel(x, c, positions, w_ada, b_ada, ffn1_norm, ffn1_w_gate, ffn1_w_up, ffn1_w_down, mix_norm, w_in, mla_q_norm, mla_w_uq, mla_kv_norm, mla_w_ukv, mla_q_gain, mla_k_gain, mla_w_o, diff_q_gain, diff_k_gain, diff_lambda_q1, diff_lambda_k1, diff_lambda_q2, diff_lambda_k2, diff_subln, diff_w_o, w_out, ffn2_norm, ffn2_w_gate, ffn2_w_up, ffn2_w_down, final_norm):
    bsz, s, d = x.shape
    depth = w_ada.shape[0]
    pos3 = positions.reshape(bsz, 1, s)
    h = x
    for l in range(depth):
        lambda_init = 0.8 - 0.6 * math.exp(-0.3 * l)
        mod = _ada(c, w_ada[l], b_ada[l]).reshape(bsz, N_MOD, d)

        h, n_mix = _ffn(h, mod, ffn1_norm[l], ffn1_w_gate[l], ffn1_w_up[l], ffn1_w_down[l],
                        sub=0, next_gain=mix_norm[l])

        w = _layer_weights(l, w_in, mla_q_norm, mla_w_uq, mla_kv_norm, mla_w_ukv, mla_q_gain,
                           mla_k_gain, diff_q_gain, diff_k_gain)
        qa, ka, va, qb, kb, vb, g = _prep(n_mix, pos3, w)
        oa = _mla_attention(ka, qa, va)
        ob = _diff_attention(kb, qb, vb,
                             diff_lambda_q1[l].reshape(1, -1), diff_lambda_k1[l].reshape(1, -1),
                             diff_lambda_q2[l].reshape(1, -1), diff_lambda_k2[l].reshape(1, -1),
                             _lane_bcast(diff_subln[l]), lambda_init)
        h = _merge(oa, ob, g, h, mod, mla_w_o[l].T.astype(BF16), diff_w_o[l].T.astype(BF16),
                   w_out[l].T.astype(BF16))

        h = _ffn(h, mod, ffn2_norm[l], ffn2_w_gate[l], ffn2_w_up[l], ffn2_w_down[l],
                 sub=2, final_gain=final_norm[l])
    return h
```

```python
import functools
import math

import jax
import jax.numpy as jnp
from jax import lax
from jax.experimental import pallas as pl
from jax.experimental.pallas import tpu as pltpu

F32 = jnp.float32
BF16 = jnp.bfloat16

NORM_EPS = 1e-6
N_MOD = 9

MLA_HEADS = 8
MLA_NOPE = 64
MLA_ROPE = 32
MLA_QK = MLA_NOPE + MLA_ROPE
MLA_V = 64
MLA_Q_LORA = 384
MLA_KV_LORA = 256
MLA_THETA = 10000.0
DIFF_HEADS = 4
DIFF_HD = 64
DIFF_V = 2 * DIFF_HD
DIFF_THETA = 500000.0
DIFF_ROT = DIFF_HD // 4

LANES = 128
HEAD_PAD = 128
SUM_ROWS = 16
MLA_VS = MLA_V + SUM_ROWS
DIFF_VS = DIFF_V + SUM_ROWS
LOG2E = math.log2(math.e)
V7X_VMEM_LIMIT = 56 * 1024 * 1024

PREP_TOKEN_TILE = 1024
PREP_SUB_TILE = 512
FFN_TOKEN_TILE = 1024
FFN_SUB_TILE = 512
MERGE_TOKEN_TILE = 1024
MERGE_SUB_TILE = 512
MLA_Q_TILE = 512
DIFF_Q_TILE = 512
MLA_KEY_CHUNK = 256
DIFF_KEY_CHUNK = 512
MLA_HEADS_PER_STEP = 2
DIFF_HEADS_PER_STEP = 2
FF_CHUNK = 256
FF_GROUP = 1
LOAD_CHUNK = 128
LOAD_SLOTS = 4
ADA_COL_BLOCK = 1024


def _sigmoid(x):
    return 1.0 / (1.0 + jnp.exp(-x))


def _rms_rows(x, gain):
    ms = jnp.mean(x * x, axis=-1, keepdims=True)
    return x * lax.rsqrt(ms + NORM_EPS) * gain


def _lane_tile(g, width):
    return jnp.tile(g, (1, width // LANES))


def _const_spec(shape):
    return pl.BlockSpec(shape, lambda *_: (0,) * len(shape), pipeline_mode=pl.Buffered(1))


def _ada_kernel(c_ref, w_ref, b_ref, o_ref):
    c = c_ref[...]
    bsz = c.shape[0]
    cond = c * _sigmoid(c)
    c_hi = cond.astype(BF16).astype(F32)
    lhs = jnp.concatenate([c_hi, cond - c_hi], axis=0).astype(BF16)
    w = w_ref[...]
    w_hi = w.astype(BF16)
    w_lo = (w - w_hi.astype(F32)).astype(BF16)
    a = jnp.dot(lhs, w_hi, preferred_element_type=F32)
    b = jnp.dot(lhs, w_lo, preferred_element_type=F32)
    o_ref[...] = a[:bsz] + a[bsz:] + b[:bsz] + b_ref[...]


def _ada(c, w_ada, b_ada):
    bsz, d = c.shape
    cols = w_ada.shape[1]
    return pl.pallas_call(
        _ada_kernel,
        grid=(cols // ADA_COL_BLOCK,),
        in_specs=[
            pl.BlockSpec((bsz, d), lambda j: (0, 0)),
            pl.BlockSpec((d, ADA_COL_BLOCK), lambda j: (0, j)),
            pl.BlockSpec((1, ADA_COL_BLOCK), lambda j: (0, j)),
        ],
        out_specs=pl.BlockSpec((bsz, ADA_COL_BLOCK), lambda j: (0, j)),
        out_shape=jax.ShapeDtypeStruct((bsz, cols), F32),
        compiler_params=pltpu.CompilerParams(dimension_semantics=("arbitrary",)),
        name="ada",
    )(c, w_ada, b_ada.reshape(1, cols))


def _adaln(x, gain, mod_ref, sub, dtype=BF16):
    shift = mod_ref[0, 3 * sub:3 * sub + 1, :]
    scale = mod_ref[0, 3 * sub + 1:3 * sub + 2, :]
    return (_rms_rows(x, gain) * (1.0 + scale) + shift).astype(dtype)


def _load_ffn_weights(wg_hbm, wu_hbm, wd_hbm, wgu_s, wd_s, stage_c, stage_r, sem):
    d_ff = wd_s.shape[0]
    jobs = []
    for r0 in range(0, wgu_s.shape[0], LOAD_CHUNK):
        jobs.append((wg_hbm, stage_c, r0, 0))
        jobs.append((wu_hbm, stage_c, r0, 1))
    for r0 in range(0, d_ff, LOAD_CHUNK):
        jobs.append((wd_hbm, stage_r, r0, None))

    def copy(i):
        src, stage, r0, _ = jobs[i]
        return pltpu.make_async_copy(src.at[r0:r0 + LOAD_CHUNK, :], stage.at[i % LOAD_SLOTS],
                                     sem.at[i % LOAD_SLOTS])

    ahead = LOAD_SLOTS - 1
    for i in range(min(ahead, len(jobs))):
        copy(i).start()
    for i, (_, stage, r0, half) in enumerate(jobs):
        if i + ahead < len(jobs):
            copy(i + ahead).start()
        copy(i).wait()
        rows = slice(r0, r0 + LOAD_CHUNK)
        slot = i % LOAD_SLOTS
        if half is None:
            wd_s[rows, :] = stage[slot].astype(BF16)
        else:
            for c0 in range(0, d_ff, FF_CHUNK):
                dst0 = 2 * c0 + half * FF_CHUNK
                wgu_s[rows, dst0:dst0 + FF_CHUNK] = stage[slot, :, c0:c0 + FF_CHUNK].astype(BF16)


def _ffn_kernel(*refs, sub, emit_next, final):
    refs = list(refs)
    x_ref, mod_ref, gain_ref = refs.pop(0), refs.pop(0), refs.pop(0)
    w_hbm = [refs.pop(0) for _ in range(3)]
    next_gain_ref = refs.pop(0) if emit_next else None
    fgain_ref = refs.pop(0) if final else None
    o_ref = refs.pop(0)
    n_next_ref = refs.pop(0) if emit_next else None
    a_ref, wgu_ref, wd_ref, stage_c, stage_r, sem = refs

    @pl.when((pl.program_id(0) == 0) & (pl.program_id(1) == 0))
    def _():
        _load_ffn_weights(*w_hbm, wgu_ref, wd_ref, stage_c, stage_r, sem)

    d_ff = wd_ref.shape[0]
    gate = mod_ref[0, 3 * sub + 2:3 * sub + 3, :]
    halves = [slice(r0, r0 + FFN_SUB_TILE) for r0 in range(0, x_ref.shape[1], FFN_SUB_TILE)]
    span = FF_CHUNK * FF_GROUP

    def up(rows):
        n = _adaln(x_ref[0, rows, :], gain_ref[...], mod_ref, sub)
        for c0 in range(0, d_ff, span):
            width = min(span, d_ff - c0)
            gu = jnp.dot(n, wgu_ref[:, 2 * c0:2 * (c0 + width)], preferred_element_type=F32)
            for j in range(0, width, FF_CHUNK):
                g = gu[:, 2 * j:2 * j + FF_CHUNK]
                u = gu[:, 2 * j + FF_CHUNK:2 * (j + FF_CHUNK)]
                a_ref[rows, c0 + j:c0 + j + FF_CHUNK] = ((g * _sigmoid(g)) * u).astype(BF16)

    def down(rows):
        f = jnp.dot(a_ref[rows, :], wd_ref[...], preferred_element_type=F32)
        h = x_ref[0, rows, :] + (0.5 * gate) * f
        if final:
            h = _rms_rows(h, fgain_ref[...])
        o_ref[0, rows, :] = h
        if emit_next:
            n_next_ref[0, :, rows] = _adaln(h, next_gain_ref[...], mod_ref, sub + 1, F32).T.astype(BF16)

    up(halves[0])
    for prev, cur in zip(halves[:-1], halves[1:]):
        up(cur)
        down(prev)
    down(halves[-1])


def _ffn(h, mod, gain, wg, wu, wd, *, sub, next_gain=None, final_gain=None):
    bsz, s, d = h.shape
    d_ff = wg.shape[1]
    tm = FFN_TOKEN_TILE
    tile = pl.BlockSpec((1, tm, d), lambda b, i: (b, i, 0))
    hbm = pl.BlockSpec(memory_space=pl.ANY)
    in_specs = [tile, pl.BlockSpec((1, N_MOD, d), lambda b, i: (b, 0, 0)), _const_spec((1, d)),
                hbm, hbm, hbm]
    args = [h, mod, gain.reshape(1, d), wg, wu, wd]
    for extra in (next_gain, final_gain):
        if extra is not None:
            in_specs.append(_const_spec((1, d)))
            args.append(extra.reshape(1, d))
    out_specs, out_shape = [tile], [jax.ShapeDtypeStruct((bsz, s, d), F32)]
    if next_gain is not None:
        out_specs.append(pl.BlockSpec((1, d, tm), lambda b, i: (b, 0, i)))
        out_shape.append(jax.ShapeDtypeStruct((bsz, d, s), BF16))
    outs = pl.pallas_call(
        functools.partial(_ffn_kernel, sub=sub, emit_next=next_gain is not None,
                          final=final_gain is not None),
        grid=(bsz, s // tm),
        in_specs=in_specs,
        out_specs=out_specs,
        out_shape=out_shape,
        scratch_shapes=[
            pltpu.VMEM((tm, d_ff), BF16),
            pltpu.VMEM((d, 2 * d_ff), BF16), pltpu.VMEM((d_ff, d), BF16),
            pltpu.VMEM((LOAD_SLOTS, LOAD_CHUNK, d_ff), F32), pltpu.VMEM((LOAD_SLOTS, LOAD_CHUNK, d), F32),
            pltpu.SemaphoreType.DMA((LOAD_SLOTS,)),
        ],
        compiler_params=pltpu.CompilerParams(
            dimension_semantics=("arbitrary", "arbitrary"),
            vmem_limit_bytes=V7X_VMEM_LIMIT),
        name="ffn%d" % sub,
    )(*args)
    return outs if next_gain is not None else outs[0]


def _rope_rows(x1, x2, cos, sin):
    return x1 * cos - x2 * sin, x2 * cos + x1 * sin


def _prep_kernel(*refs):
    for t0 in range(0, refs[0].shape[2], PREP_SUB_TILE):
        _prep_group(slice(t0, t0 + PREP_SUB_TILE), *refs)


def _prep_group(tok, n_ref, pos_ref, win_ref, wuq_ref, wukv_ref,
                qnorm_ref, kvnorm_ref, qgain_a_ref, kgain_a_ref, qgain_b_ref, kgain_b_ref,
                freq_a_ref, freq_b_ref,
                qa_ref, ka_ref, va_ref, qb_ref, kb_ref, vb_ref, g_ref):
    tm = PREP_SUB_TILE
    n_t = n_ref[0, :, tok]

    def proj_t(r0, r1):
        return jnp.dot(win_ref[r0:r1, :], n_t, preferred_element_type=F32)

    o_q, o_kv, o_kr = 0, MLA_Q_LORA, MLA_Q_LORA + MLA_KV_LORA
    o_qb = o_kr + MLA_ROPE
    w_b = DIFF_HEADS * DIFF_V
    o_kb, o_vb, o_g = o_qb + w_b, o_qb + 2 * w_b, o_qb + 3 * w_b
    g_chunk = g_ref.shape[1] // 4

    def emit_gates(i):
        r0 = i * g_chunk
        z = proj_t(o_g + r0, o_g + r0 + g_chunk)
        g_ref[0, r0:r0 + g_chunk, tok] = _sigmoid(z).astype(BF16)

    z_a = proj_t(o_q, o_qb)
    emit_gates(0)
    zq, zkv, kr = z_a[o_q:o_kv], z_a[o_kv:o_kr], z_a[o_kr:o_qb]
    rq = lax.rsqrt(jnp.mean(zq * zq, axis=0, keepdims=True) + NORM_EPS)
    zqn = (zq * rq * _lane_tile(qnorm_ref[...], tm)).astype(BF16)
    q_all = jnp.dot(wuq_ref[...], zqn, preferred_element_type=F32)
    rkv = lax.rsqrt(jnp.mean(zkv * zkv, axis=0, keepdims=True) + NORM_EPS)
    zkvn = (zkv * rkv * _lane_tile(kvnorm_ref[...], tm)).astype(BF16)
    kv_all = jnp.dot(wukv_ref[...], zkvn, preferred_element_type=F32)
    emit_gates(1)

    pos = pos_ref[0, :, tok].astype(F32)
    ang_a = pos * _lane_tile(freq_a_ref[...], tm)
    cos_a, sin_a = jnp.cos(ang_a), jnp.sin(ang_a)
    ang_b = pos * _lane_tile(freq_b_ref[...], tm)
    cos_b, sin_b = jnp.cos(ang_b), jnp.sin(ang_b)

    qgain_a = _lane_tile(qgain_a_ref[...], tm)
    kgain_a = _lane_tile(kgain_a_ref[...], tm)
    q_scale = LOG2E / math.sqrt(MLA_QK)
    pad_rows = jnp.zeros((HEAD_PAD - MLA_QK, tm), F32)
    sum_rows = (lax.broadcasted_iota(jnp.int32, (SUM_ROWS, tm), 0) == 0).astype(BF16)
    half = MLA_ROPE // 2

    def head_norm_rope_a(xh, gain):
        r = lax.rsqrt(jnp.sum(xh * xh, axis=0, keepdims=True) * (1.0 / MLA_QK) + NORM_EPS)
        xh = xh * r * gain
        r1, r2 = _rope_rows(xh[MLA_NOPE:MLA_NOPE + half], xh[MLA_NOPE + half:MLA_QK], cos_a, sin_a)
        return jnp.concatenate([xh[:MLA_NOPE], r1, r2, xh[MLA_QK:]], axis=0)

    zqb = proj_t(o_qb, o_kb)
    zkb = proj_t(o_kb, o_vb)
    for hd in range(MLA_HEADS):
        lo = hd * HEAD_PAD
        qh = head_norm_rope_a(q_all[lo:lo + HEAD_PAD], qgain_a) * q_scale
        qa_ref[0, lo:lo + HEAD_PAD, tok] = qh.astype(BF16)
        kvh = kv_all[lo:lo + HEAD_PAD]
        va_ref[0, hd * MLA_VS:hd * MLA_VS + MLA_V, tok] = kvh[MLA_NOPE:].astype(BF16)
        va_ref[0, hd * MLA_VS + MLA_V:(hd + 1) * MLA_VS, tok] = sum_rows
        kh = jnp.concatenate([kvh[:MLA_NOPE], kr, pad_rows], axis=0)
        kh = head_norm_rope_a(kh, kgain_a)
        ka_ref[0, tok, lo:lo + HEAD_PAD] = kh.T.astype(BF16)
    emit_gates(2)

    qgain_b = _lane_tile(qgain_b_ref[...], tm)
    kgain_b = _lane_tile(kgain_b_ref[...], tm)
    qb_scale = LOG2E / math.sqrt(DIFF_HD)
    hb = DIFF_ROT // 2

    def head_norm_rope_b(xh, gain):
        r = lax.rsqrt(jnp.mean(xh * xh, axis=0, keepdims=True) + NORM_EPS)
        xh = xh * r * gain
        r1, r2 = _rope_rows(xh[:hb], xh[hb:DIFF_ROT], cos_b, sin_b)
        return jnp.concatenate([r1, r2, xh[DIFF_ROT:]], axis=0)

    zvb = proj_t(o_vb, o_g)
    k_parts = []
    for blk in range(2 * DIFF_HEADS):
        lo = blk * DIFF_HD
        qh = head_norm_rope_b(zqb[lo:lo + DIFF_HD], qgain_b) * qb_scale
        qb_ref[0, lo:lo + DIFF_HD, tok] = qh.astype(BF16)
        k_parts.append(head_norm_rope_b(zkb[lo:lo + DIFF_HD], kgain_b))
    for hd in range(DIFF_HEADS):
        k12 = jnp.concatenate(k_parts[2 * hd:2 * hd + 2], axis=0)
        kb_ref[0, tok, hd * DIFF_V:(hd + 1) * DIFF_V] = k12.T.astype(BF16)
        vb_ref[0, hd * DIFF_VS:hd * DIFF_VS + DIFF_V, tok] = zvb[hd * DIFF_V:(hd + 1) * DIFF_V].astype(BF16)
        vb_ref[0, hd * DIFF_VS + DIFF_V:(hd + 1) * DIFF_VS, tok] = sum_rows
    emit_gates(3)


def _prep(n_t, pos3, w):
    bsz, d, s = n_t.shape
    tm = PREP_TOKEN_TILE
    n_in = w["win_t"].shape[0]
    wa, wb = MLA_HEADS * HEAD_PAD, DIFF_HEADS * DIFF_V
    n_gate = n_in - (MLA_Q_LORA + MLA_KV_LORA + MLA_ROPE + 3 * wb)

    def fm(rows):
        return (jax.ShapeDtypeStruct((bsz, rows, s), BF16),
                pl.BlockSpec((1, rows, tm), lambda b, i: (b, 0, i)))

    def tmaj(cols):
        return (jax.ShapeDtypeStruct((bsz, s, cols), BF16),
                pl.BlockSpec((1, tm, cols), lambda b, i: (b, i, 0)))

    outs = [fm(wa), tmaj(wa), fm(MLA_HEADS * MLA_VS), fm(wb), tmaj(wb), fm(DIFF_HEADS * DIFF_VS), fm(n_gate)]
    consts = [w["win_t"], w["wuq_t"], w["wukv_t"], w["qnorm"], w["kvnorm"],
              w["qgain_a"], w["kgain_a"], w["qgain_b"], w["kgain_b"], w["freq_a"], w["freq_b"]]
    in_specs = [
        pl.BlockSpec((1, d, tm), lambda b, i: (b, 0, i)),
        pl.BlockSpec((1, 1, tm), lambda b, i: (b, 0, i)),
    ] + [_const_spec(a.shape) for a in consts]
    return pl.pallas_call(
        _prep_kernel,
        grid=(bsz, s // tm),
        in_specs=in_specs,
        out_specs=[o[1] for o in outs],
        out_shape=[o[0] for o in outs],
        compiler_params=pltpu.CompilerParams(
            dimension_semantics=("arbitrary", "arbitrary"),
            vmem_limit_bytes=V7X_VMEM_LIMIT),
        name="prep",
    )(n_t, pos3, *consts)


def _normalise(acc, rows):
    return acc[:rows] * (1.0 / acc[rows:rows + 1])


def _attention_pipeline(n_heads, s_len, tq, kc, n_streams, scores_fn, v_fn, finish_fn):
    units = [(hd, q0, c0) for hd in range(n_heads) for q0 in range(0, s_len, tq)
             for c0 in range(0, s_len, kc)]
    streams = range(n_streams)
    n = len(units)
    s_cur = [scores_fn(*units[0], t) for t in streams]
    m = [None] * n_streams
    acc = [None] * n_streams
    pending = None
    for i in range(n + 1):
        s_nxt = [scores_fn(*units[i + 1], t) for t in streams] if i + 1 < n else None
        if pending is not None:
            (hd, q0, c0), alphas, probs = pending
            v = v_fn(hd, c0)
            for t in streams:
                o = jnp.dot(v, probs[t], preferred_element_type=F32)
                acc[t] = o if alphas[t] is None else alphas[t] * acc[t] + o
            if c0 + kc == s_len:
                finish_fn(hd, q0, acc)
                acc = [None] * n_streams
            pending = None
        if i < n:
            _, _, c0 = units[i]
            alphas, probs = [], []
            for t in streams:
                cmax = jnp.max(s_cur[t], axis=0, keepdims=True)
                if c0 == 0:
                    m[t] = cmax
                    alphas.append(None)
                else:
                    m_new = jnp.maximum(m[t], cmax)
                    alphas.append(jnp.exp2(m[t] - m_new))
                    m[t] = m_new
                probs.append(jnp.exp2(s_cur[t] - m[t]).astype(BF16))
            pending = (units[i], alphas, probs)
            s_cur = s_nxt


def _mla_kernel(k_ref, q_ref, v_ref, o_ref):
    def scores(hd, q0, c0, t):
        return jnp.dot(k_ref[0, c0:c0 + MLA_KEY_CHUNK, hd * HEAD_PAD:(hd + 1) * HEAD_PAD],
                       q_ref[0, hd * HEAD_PAD:(hd + 1) * HEAD_PAD, q0:q0 + MLA_Q_TILE],
                       preferred_element_type=F32)

    def values(hd, c0):
        return v_ref[0, hd * MLA_VS:(hd + 1) * MLA_VS, c0:c0 + MLA_KEY_CHUNK]

    def finish(hd, q0, accs):
        o_ref[0, hd * MLA_V:(hd + 1) * MLA_V, q0:q0 + MLA_Q_TILE] = (
            _normalise(accs[0], MLA_V).astype(BF16))

    _attention_pipeline(MLA_HEADS_PER_STEP, k_ref.shape[1], MLA_Q_TILE, MLA_KEY_CHUNK, 1,
                        scores, values, finish)


def _mla_attention(ka, qa, va):
    bsz, s, _ = ka.shape
    hp = MLA_HEADS_PER_STEP
    return pl.pallas_call(
        _mla_kernel,
        grid=(bsz, MLA_HEADS // hp),
        in_specs=[
            pl.BlockSpec((1, s, hp * HEAD_PAD), lambda b, h: (b, 0, h)),
            pl.BlockSpec((1, hp * HEAD_PAD, s), lambda b, h: (b, h, 0)),
            pl.BlockSpec((1, hp * MLA_VS, s), lambda b, h: (b, h, 0)),
        ],
        out_specs=pl.BlockSpec((1, hp * MLA_V, s), lambda b, h: (b, h, 0)),
        out_shape=jax.ShapeDtypeStruct((bsz, MLA_HEADS * MLA_V, s), BF16),
        compiler_params=pltpu.CompilerParams(
            dimension_semantics=("arbitrary", "arbitrary"),
            vmem_limit_bytes=V7X_VMEM_LIMIT),
        name="mla_attn",
    )(ka, qa, va)


def _diff_kernel(k_ref, q_ref, v_ref, lq1_ref, lk1_ref, lq2_ref, lk2_ref, subln_ref, o_ref, *,
                 lambda_init):
    lam = (jnp.exp(jnp.sum(lq1_ref[...] * lk1_ref[...], axis=-1, keepdims=True))
           - jnp.exp(jnp.sum(lq2_ref[...] * lk2_ref[...], axis=-1, keepdims=True))
           + lambda_init)
    subln = _lane_tile(subln_ref[...], DIFF_Q_TILE)
    zeros = jnp.zeros((DIFF_HD, DIFF_Q_TILE), BF16)

    def scores(hd, q0, c0, t):
        lo = hd * DIFF_V
        q12 = q_ref[0, lo:lo + DIFF_V, q0:q0 + DIFF_Q_TILE]
        k12 = k_ref[0, c0:c0 + DIFF_KEY_CHUNK, lo:lo + DIFF_V]
        if t == 0:
            q = jnp.concatenate([q12[:DIFF_HD], zeros], axis=0)
        else:
            q = jnp.concatenate([zeros, q12[DIFF_HD:]], axis=0)
        return jnp.dot(k12, q, preferred_element_type=F32)

    def values(hd, c0):
        return v_ref[0, hd * DIFF_VS:(hd + 1) * DIFF_VS, c0:c0 + DIFF_KEY_CHUNK]

    def finish(hd, q0, accs):
        o = _normalise(accs[0], DIFF_V) - lam * _normalise(accs[1], DIFF_V)
        r = lax.rsqrt(jnp.mean(o * o, axis=0, keepdims=True) + NORM_EPS)
        o_ref[0, hd * DIFF_V:(hd + 1) * DIFF_V, q0:q0 + DIFF_Q_TILE] = (
            (o * r * subln) * (1.0 - lambda_init)).astype(BF16)

    _attention_pipeline(DIFF_HEADS_PER_STEP, k_ref.shape[1], DIFF_Q_TILE, DIFF_KEY_CHUNK, 2,
                        scores, values, finish)


def _diff_attention(kb, qb, vb, lq1, lk1, lq2, lk2, subln, lambda_init):
    bsz, s, _ = kb.shape
    hp = DIFF_HEADS_PER_STEP
    vec = _const_spec((1, DIFF_HD))
    return pl.pallas_call(
        functools.partial(_diff_kernel, lambda_init=lambda_init),
        grid=(bsz, DIFF_HEADS // hp),
        in_specs=[
            pl.BlockSpec((1, s, hp * DIFF_V), lambda b, h: (b, 0, h)),
            pl.BlockSpec((1, hp * DIFF_V, s), lambda b, h: (b, h, 0)),
            pl.BlockSpec((1, hp * DIFF_VS, s), lambda b, h: (b, h, 0)),
            vec, vec, vec, vec,
            _const_spec((DIFF_V, LANES)),
        ],
        out_specs=pl.BlockSpec((1, hp * DIFF_V, s), lambda b, h: (b, h, 0)),
        out_shape=jax.ShapeDtypeStruct((bsz, DIFF_HEADS * DIFF_V, s), BF16),
        compiler_params=pltpu.CompilerParams(
            dimension_semantics=("arbitrary", "arbitrary"),
            vmem_limit_bytes=V7X_VMEM_LIMIT),
        name="diff_attn",
    )(kb, qb, vb, lq1, lk1, lq2, lk2, subln)


def _merge_kernel(oa_ref, ob_ref, g_ref, h_ref, mod_ref, woa_ref, wob_ref, wout_ref, o_ref):
    d = h_ref.shape[2]
    gate = mod_ref[0, 5:6, :]
    groups = [slice(t0, t0 + MERGE_SUB_TILE) for t0 in range(0, h_ref.shape[1], MERGE_SUB_TILE)]

    def branch_mix(tok):
        ya = jnp.dot(woa_ref[...], oa_ref[0, :, tok], preferred_element_type=F32)
        yb = jnp.dot(wob_ref[...], ob_ref[0, :, tok], preferred_element_type=F32)
        ga = g_ref[0, :d, tok].astype(F32)
        gb = g_ref[0, d:, tok].astype(F32)
        return (ga * ya + gb * yb).astype(BF16)

    def project(tok, mix):
        y_t = jnp.dot(wout_ref[...], mix, preferred_element_type=F32)
        o_ref[0, tok, :] = h_ref[0, tok, :] + gate * y_t.T

    mixes = [branch_mix(groups[0])]
    for prev, cur in zip(groups[:-1], groups[1:]):
        mixes.append(branch_mix(cur))
        project(prev, mixes[-2])
    project(groups[-1], mixes[-1])


def _merge(oa, ob, g, h, mod, woa_t, wob_t, wout_t):
    bsz, s, d = h.shape
    tm = MERGE_TOKEN_TILE
    return pl.pallas_call(
        _merge_kernel,
        grid=(bsz, s // tm),
        in_specs=[
            pl.BlockSpec((1, oa.shape[1], tm), lambda b, i: (b, 0, i)),
            pl.BlockSpec((1, ob.shape[1], tm), lambda b, i: (b, 0, i)),
            pl.BlockSpec((1, g.shape[1], tm), lambda b, i: (b, 0, i)),
            pl.BlockSpec((1, tm, d), lambda b, i: (b, i, 0)),
            pl.BlockSpec((1, N_MOD, d), lambda b, i: (b, 0, 0)),
            _const_spec(woa_t.shape),
            _const_spec(wob_t.shape),
            _const_spec(wout_t.shape),
        ],
        out_specs=pl.BlockSpec((1, tm, d), lambda b, i: (b, i, 0)),
        out_shape=jax.ShapeDtypeStruct((bsz, s, d), F32),
        compiler_params=pltpu.CompilerParams(
            dimension_semantics=("arbitrary", "arbitrary"),
            vmem_limit_bytes=V7X_VMEM_LIMIT),
        name="merge",
    )(oa, ob, g, h, mod, woa_t, wob_t, wout_t)


def _lane_bcast(v, rows=None):
    n = v.shape[0]
    out = jnp.broadcast_to(v.astype(F32)[:, None], (n, LANES))
    if rows is not None and rows > n:
        out = jnp.pad(out, ((0, rows - n), (0, 0)))
    return out


def _rope_freqs(half, theta):
    return 1.0 / (theta ** (jnp.arange(half, dtype=F32) / half))


def _layer_weights(l, w_in, mla_q_norm, mla_w_uq, mla_kv_norm, mla_w_ukv, mla_q_gain, mla_k_gain,
                   diff_q_gain, diff_k_gain):
    wuq = mla_w_uq[l].reshape(MLA_Q_LORA, MLA_HEADS, MLA_QK)
    wuq = jnp.pad(wuq, ((0, 0), (0, 0), (0, HEAD_PAD - MLA_QK))).reshape(MLA_Q_LORA, -1)
    return {
        "win_t": w_in[l].T.astype(BF16),
        "wuq_t": wuq.T.astype(BF16),
        "wukv_t": mla_w_ukv[l].T.astype(BF16),
        "qnorm": _lane_bcast(mla_q_norm[l]),
        "kvnorm": _lane_bcast(mla_kv_norm[l]),
        "qgain_a": _lane_bcast(mla_q_gain[l], HEAD_PAD),
        "kgain_a": _lane_bcast(mla_k_gain[l], HEAD_PAD),
        "qgain_b": _lane_bcast(diff_q_gain[l]),
        "kgain_b": _lane_bcast(diff_k_gain[l]),
        "freq_a": _lane_bcast(_rope_freqs(MLA_ROPE // 2, MLA_THETA)),
        "freq_b": _lane_bcast(_rope_freqs(DIFF_ROT // 2, DIFF_THETA)),
    }


def kernel(x, c, positions, w_ada, b_ada, ffn1_norm, ffn1_w_gate, ffn1_w_up, ffn1_w_down, mix_norm, w_in, mla_q_norm, mla_w_uq, mla_kv_norm, mla_w_ukv, mla_q_gain, mla_k_gain, mla_w_o, diff_q_gain, diff_k_gain, diff_lambda_q1, diff_lambda_k1, diff_lambda_q2, diff_lambda_k2, diff_subln, diff_w_o, w_out, ffn2_norm, ffn2_w_gate, ffn2_w_up, ffn2_w_down, final_norm):
    bsz, s, d = x.shape
    depth = w_ada.shape[0]
    pos3 = positions.reshape(bsz, 1, s)
    h = x
    for l in range(depth):
        lambda_init = 0.8 - 0.6 * math.exp(-0.3 * l)
        mod = _ada(c, w_ada[l], b_ada[l]).reshape(bsz, N_MOD, d)

        h, n_mix = _ffn(h, mod, ffn1_norm[l], ffn1_w_gate[l], ffn1_w_up[l], ffn1_w_down[l],
                        sub=0, next_gain=mix_norm[l])

        w = _layer_weights(l, w_in, mla_q_norm, mla_w_uq, mla_kv_norm, mla_w_ukv, mla_q_gain,
                           mla_k_gain, diff_q_gain, diff_k_gain)
        qa, ka, va, qb, kb, vb, g = _prep(n_mix, pos3, w)
        oa = _mla_attention(ka, qa, va)
        ob = _diff_attention(kb, qb, vb,
                             diff_lambda_q1[l].reshape(1, -1), diff_lambda_k1[l].reshape(1, -1),
                             diff_lambda_q2[l].reshape(1, -1), diff_lambda_k2[l].reshape(1, -1),
                             _lane_bcast(diff_subln[l]), lambda_init)
        h = _merge(oa, ob, g, h, mod, mla_w_o[l].T.astype(BF16), diff_w_o[l].T.astype(BF16),
                   w_out[l].T.astype(BF16))

        h = _ffn(h, mod, ffn2_norm[l], ffn2_w_gate[l], ffn2_w_up[l], ffn2_w_down[l],
                 sub=2, final_gain=final_norm[l])
    return h
```

```python
import functools
import math

import jax
import jax.numpy as jnp
from jax import lax
from jax.experimental import pallas as pl
from jax.experimental.pallas import tpu as pltpu

F32 = jnp.float32
BF16 = jnp.bfloat16

NORM_EPS = 1e-6
N_MOD = 9

MLA_HEADS = 8
MLA_NOPE = 64
MLA_ROPE = 32
MLA_QK = MLA_NOPE + MLA_ROPE
MLA_V = 64
MLA_Q_LORA = 384
MLA_KV_LORA = 256
MLA_THETA = 10000.0
DIFF_HEADS = 4
DIFF_HD = 64
DIFF_V = 2 * DIFF_HD
DIFF_THETA = 500000.0
DIFF_ROT = DIFF_HD // 4

LANES = 128
HEAD_PAD = 128
SUM_ROWS = 16
MLA_VS = MLA_V + SUM_ROWS
DIFF_VS = DIFF_V + SUM_ROWS
LOG2E = math.log2(math.e)
V7X_VMEM_LIMIT = 56 * 1024 * 1024

def _table_layout(sizes):
    layout, lo = {}, 0
    for name, rows in sizes:
        layout[name] = (lo, rows)
        lo += rows
    return layout


PREP_TABLE = _table_layout([
    ("qnorm", MLA_Q_LORA), ("kvnorm", MLA_KV_LORA), ("qgain_a", HEAD_PAD), ("kgain_a", HEAD_PAD),
    ("qgain_b", DIFF_HD), ("kgain_b", DIFF_HD), ("freq_a", MLA_ROPE // 2), ("freq_b", DIFF_ROT // 2)])

PREP_TOKEN_TILE = 1024
PREP_SUB_TILE = 512
FFN_TOKEN_TILE = 1024
FFN_SUB_TILE = 512
MERGE_TOKEN_TILE = 1024
MERGE_SUB_TILE = 512
MLA_Q_TILE = 512
DIFF_Q_TILE = 512
MLA_KEY_CHUNK = 256
DIFF_KEY_CHUNK = 512
MLA_HEADS_PER_STEP = 2
DIFF_HEADS_PER_STEP = 2
FF_CHUNK = 256
LOAD_SLOTS = 4
ADA_COL_BLOCK = 2304


def _sigmoid(x):
    return 1.0 / (1.0 + jnp.exp(-x))


def _rms_rows(x, gain):
    ms = jnp.mean(x * x, axis=-1, keepdims=True)
    return x * lax.rsqrt(ms + NORM_EPS) * gain


def _lane_tile(g, width):
    return jnp.tile(g, (1, width // LANES))


def _const_spec(shape):
    return pl.BlockSpec(shape, lambda *_: (0,) * len(shape), pipeline_mode=pl.Buffered(1))


def _ada_kernel(c_ref, w_ref, b_ref, o_ref):
    c = c_ref[...]
    bsz = c.shape[0]
    cond = c * _sigmoid(c)
    c_hi = cond.astype(BF16).astype(F32)
    lhs = jnp.concatenate([c_hi, cond - c_hi], axis=0).astype(BF16)
    w = w_ref[...]
    w_hi = w.astype(BF16)
    w_lo = (w - w_hi.astype(F32)).astype(BF16)
    a = jnp.dot(lhs, w_hi, preferred_element_type=F32)
    b = jnp.dot(lhs, w_lo, preferred_element_type=F32)
    o_ref[...] = a[:bsz] + a[bsz:] + b[:bsz] + b_ref[...]


def _ada(c, w_ada, b_ada):
    bsz, d = c.shape
    cols = w_ada.shape[1]
    return pl.pallas_call(
        _ada_kernel,
        grid=(cols // ADA_COL_BLOCK,),
        in_specs=[
            pl.BlockSpec((bsz, d), lambda j: (0, 0)),
            pl.BlockSpec((d, ADA_COL_BLOCK), lambda j: (0, j)),
            pl.BlockSpec((1, ADA_COL_BLOCK), lambda j: (0, j)),
        ],
        out_specs=pl.BlockSpec((bsz, ADA_COL_BLOCK), lambda j: (0, j)),
        out_shape=jax.ShapeDtypeStruct((bsz, cols), F32),
        compiler_params=pltpu.CompilerParams(dimension_semantics=("arbitrary",),
                                             vmem_limit_bytes=V7X_VMEM_LIMIT),
        name="ada",
    )(c, w_ada, b_ada.reshape(1, cols))


def _adaln(x, gain, mod_ref, sub, dtype=BF16):
    shift = mod_ref[0, 3 * sub:3 * sub + 1, :]
    scale = mod_ref[0, 3 * sub + 1:3 * sub + 2, :]
    return (_rms_rows(x, gain) * (1.0 + scale) + shift).astype(dtype)


class _WeightStream:
    def __init__(self, wg_hbm, wu_hbm, wd_hbm, wgu_s, wd_s, stage_c, stage_r, sem):
        self.sem = sem
        self.jobs = []
        for c0 in range(0, wd_s.shape[0], FF_CHUNK):
            blk = slice(c0, c0 + FF_CHUNK)
            self.jobs.append((wg_hbm.at[:, blk], stage_c, wgu_s.at[:, 2 * c0:2 * c0 + FF_CHUNK]))
            self.jobs.append((wu_hbm.at[:, blk], stage_c,
                              wgu_s.at[:, 2 * c0 + FF_CHUNK:2 * (c0 + FF_CHUNK)]))
            self.jobs.append((wd_hbm.at[blk, :], stage_r, wd_s.at[blk, :]))
        self.landed = 0
        for i in range(min(LOAD_SLOTS - 1, len(self.jobs))):
            self._copy(i).start()

    def _copy(self, i):
        src, stage, _ = self.jobs[i]
        return pltpu.make_async_copy(src, stage.at[i % LOAD_SLOTS], self.sem.at[i % LOAD_SLOTS])

    def land(self, count):
        for i in range(self.landed, self.landed + count):
            if i + LOAD_SLOTS - 1 < len(self.jobs):
                self._copy(i + LOAD_SLOTS - 1).start()
            self._copy(i).wait()
            _, stage, dst = self.jobs[i]
            dst[...] = stage[i % LOAD_SLOTS].astype(BF16)
        self.landed += count


def _ffn_kernel(*refs, sub, emit_next, final):
    refs = list(refs)
    x_ref, mod_ref, gain_ref = refs.pop(0), refs.pop(0), refs.pop(0)
    w_hbm = [refs.pop(0) for _ in range(3)]
    next_gain_ref = refs.pop(0) if emit_next else None
    fgain_ref = refs.pop(0) if final else None
    o_ref = refs.pop(0)
    n_next_ref = refs.pop(0) if emit_next else None
    a_ref, wgu_ref, wd_ref, stage_c, stage_r, sem = refs

    d_ff = wd_ref.shape[0]
    gate = mod_ref[0, 3 * sub + 2:3 * sub + 3, :]
    halves = [slice(r0, r0 + FFN_SUB_TILE) for r0 in range(0, x_ref.shape[1], FFN_SUB_TILE)]
    blocks = range(0, d_ff, FF_CHUNK)
    first = (pl.program_id(0) == 0) & (pl.program_id(1) == 0)

    def norm(rows):
        return _adaln(x_ref[0, rows, :], gain_ref[...], mod_ref, sub)

    def up_block(n, rows, c0):
        gu = jnp.dot(n, wgu_ref[:, 2 * c0:2 * (c0 + FF_CHUNK)], preferred_element_type=F32)
        g, u = gu[:, :FF_CHUNK], gu[:, FF_CHUNK:]
        a_ref[rows, c0:c0 + FF_CHUNK] = ((g * _sigmoid(g)) * u).astype(BF16)

    def up(rows):
        n = norm(rows)
        for c0 in blocks:
            up_block(n, rows, c0)

    def down(rows):
        f = jnp.dot(a_ref[rows, :], wd_ref[...], preferred_element_type=F32)
        h = x_ref[0, rows, :] + (0.5 * gate) * f
        if final:
            h = _rms_rows(h, fgain_ref[...])
        o_ref[0, rows, :] = h
        if emit_next:
            n_next_ref[0, :, rows] = _adaln(h, next_gain_ref[...], mod_ref, sub + 1, F32).T.astype(BF16)

    @pl.when(first)
    def _():
        stream = _WeightStream(*w_hbm, wgu_ref, wd_ref, stage_c, stage_r, sem)
        norms = [norm(rows) for rows in halves]
        for c0 in blocks:
            stream.land(3)
            for n, rows in zip(norms, halves):
                up_block(n, rows, c0)
        for rows in halves:
            down(rows)

    @pl.when(jnp.logical_not(first))
    def _():
        up(halves[0])
        for prev, cur in zip(halves[:-1], halves[1:]):
            up(cur)
            down(prev)
        down(halves[-1])


def _ffn(h, mod, gain, wg, wu, wd, *, sub, next_gain=None, final_gain=None):
    bsz, s, d = h.shape
    d_ff = wg.shape[1]
    tm = FFN_TOKEN_TILE
    tile = pl.BlockSpec((1, tm, d), lambda b, i: (b, i, 0))
    hbm = pl.BlockSpec(memory_space=pl.ANY)
    in_specs = [tile, pl.BlockSpec((1, N_MOD, d), lambda b, i: (b, 0, 0)), _const_spec((1, d)),
                hbm, hbm, hbm]
    args = [h, mod, gain.reshape(1, d), wg, wu, wd]
    for extra in (next_gain, final_gain):
        if extra is not None:
            in_specs.append(_const_spec((1, d)))
            args.append(extra.reshape(1, d))
    out_specs, out_shape = [tile], [jax.ShapeDtypeStruct((bsz, s, d), F32)]
    if next_gain is not None:
        out_specs.append(pl.BlockSpec((1, d, tm), lambda b, i: (b, 0, i)))
        out_shape.append(jax.ShapeDtypeStruct((bsz, d, s), BF16))
    outs = pl.pallas_call(
        functools.partial(_ffn_kernel, sub=sub, emit_next=next_gain is not None,
                          final=final_gain is not None),
        grid=(bsz, s // tm),
        in_specs=in_specs,
        out_specs=out_specs,
        out_shape=out_shape,
        scratch_shapes=[
            pltpu.VMEM((tm, d_ff), BF16),
            pltpu.VMEM((d, 2 * d_ff), BF16), pltpu.VMEM((d_ff, d), BF16),
            pltpu.VMEM((LOAD_SLOTS, d, FF_CHUNK), F32), pltpu.VMEM((LOAD_SLOTS, FF_CHUNK, d), F32),
            pltpu.SemaphoreType.DMA((LOAD_SLOTS,)),
        ],
        compiler_params=pltpu.CompilerParams(
            dimension_semantics=("arbitrary", "arbitrary"),
            vmem_limit_bytes=V7X_VMEM_LIMIT),
        name="ffn%d" % sub,
    )(*args)
    return outs if next_gain is not None else outs[0]


def _rope_rows(x1, x2, cos, sin):
    return x1 * cos - x2 * sin, x2 * cos + x1 * sin


def _prep_kernel(*refs):
    for t0 in range(0, refs[0].shape[2], PREP_SUB_TILE):
        _prep_group(slice(t0, t0 + PREP_SUB_TILE), *refs)


def _prep_group(tok, n_ref, pos_ref, win_ref, wuq_ref, wukv_ref,
                tab_ref,
                qa_ref, ka_ref, va_ref, qb_ref, kb_ref, vb_ref, g_ref):
    tm = PREP_SUB_TILE
    n_t = n_ref[0, :, tok]

    def row_consts(name):
        lo, rows = PREP_TABLE[name]
        return _lane_tile(tab_ref[lo:lo + rows, :], tm)

    def proj_t(r0, r1):
        return jnp.dot(win_ref[r0:r1, :], n_t, preferred_element_type=F32)

    o_q, o_kv, o_kr = 0, MLA_Q_LORA, MLA_Q_LORA + MLA_KV_LORA
    o_qb = o_kr + MLA_ROPE
    w_b = DIFF_HEADS * DIFF_V
    o_kb, o_vb, o_g = o_qb + w_b, o_qb + 2 * w_b, o_qb + 3 * w_b
    g_chunk = g_ref.shape[1] // 4

    def emit_gates(i):
        r0 = i * g_chunk
        z = proj_t(o_g + r0, o_g + r0 + g_chunk)
        g_ref[0, r0:r0 + g_chunk, tok] = _sigmoid(z).astype(BF16)

    z_a = proj_t(o_q, o_qb)
    emit_gates(0)
    zq, zkv, kr = z_a[o_q:o_kv], z_a[o_kv:o_kr], z_a[o_kr:o_qb]
    rq = lax.rsqrt(jnp.mean(zq * zq, axis=0, keepdims=True) + NORM_EPS)
    zqn = (zq * rq * row_consts("qnorm")).astype(BF16)
    q_all = jnp.dot(wuq_ref[...], zqn, preferred_element_type=F32)
    rkv = lax.rsqrt(jnp.mean(zkv * zkv, axis=0, keepdims=True) + NORM_EPS)
    zkvn = (zkv * rkv * row_consts("kvnorm")).astype(BF16)
    kv_all = jnp.dot(wukv_ref[...], zkvn, preferred_element_type=F32)
    emit_gates(1)

    pos = pos_ref[0, :, tok].astype(F32)
    ang_a = pos * row_consts("freq_a")
    cos_a, sin_a = jnp.cos(ang_a), jnp.sin(ang_a)
    ang_b = pos * row_consts("freq_b")
    cos_b, sin_b = jnp.cos(ang_b), jnp.sin(ang_b)

    qgain_a = row_consts("qgain_a")
    kgain_a = row_consts("kgain_a")
    q_scale = LOG2E / math.sqrt(MLA_QK)
    pad_rows = jnp.zeros((HEAD_PAD - MLA_QK, tm), F32)
    sum_rows = (lax.broadcasted_iota(jnp.int32, (SUM_ROWS, tm), 0) == 0).astype(BF16)
    half = MLA_ROPE // 2

    def head_norm_rope_a(xh, gain):
        r = lax.rsqrt(jnp.sum(xh * xh, axis=0, keepdims=True) * (1.0 / MLA_QK) + NORM_EPS)
        xh = xh * r * gain
        r1, r2 = _rope_rows(xh[MLA_NOPE:MLA_NOPE + half], xh[MLA_NOPE + half:MLA_QK], cos_a, sin_a)
        return jnp.concatenate([xh[:MLA_NOPE], r1, r2, xh[MLA_QK:]], axis=0)

    zqb = proj_t(o_qb, o_kb)
    zkb = proj_t(o_kb, o_vb)
    for hd in range(MLA_HEADS):
        lo = hd * HEAD_PAD
        qh = head_norm_rope_a(q_all[lo:lo + HEAD_PAD], qgain_a) * q_scale
        qa_ref[0, lo:lo + HEAD_PAD, tok] = qh.astype(BF16)
        kvh = kv_all[lo:lo + HEAD_PAD]
        va_ref[0, hd * MLA_VS:hd * MLA_VS + MLA_V, tok] = kvh[MLA_NOPE:].astype(BF16)
        va_ref[0, hd * MLA_VS + MLA_V:(hd + 1) * MLA_VS, tok] = sum_rows
        kh = jnp.concatenate([kvh[:MLA_NOPE], kr, pad_rows], axis=0)
        kh = head_norm_rope_a(kh, kgain_a)
        ka_ref[0, tok, lo:lo + HEAD_PAD] = kh.T.astype(BF16)
    emit_gates(2)

    qgain_b = row_consts("qgain_b")
    kgain_b = row_consts("kgain_b")
    qb_scale = LOG2E / math.sqrt(DIFF_HD)
    hb = DIFF_ROT // 2

    def head_norm_rope_b(xh, gain):
        r = lax.rsqrt(jnp.mean(xh * xh, axis=0, keepdims=True) + NORM_EPS)
        xh = xh * r * gain
        r1, r2 = _rope_rows(xh[:hb], xh[hb:DIFF_ROT], cos_b, sin_b)
        return jnp.concatenate([r1, r2, xh[DIFF_ROT:]], axis=0)

    zvb = proj_t(o_vb, o_g)
    k_parts = []
    for blk in range(2 * DIFF_HEADS):
        lo = blk * DIFF_HD
        qh = head_norm_rope_b(zqb[lo:lo + DIFF_HD], qgain_b) * qb_scale
        qb_ref[0, lo:lo + DIFF_HD, tok] = qh.astype(BF16)
        k_parts.append(head_norm_rope_b(zkb[lo:lo + DIFF_HD], kgain_b))
    for hd in range(DIFF_HEADS):
        k12 = jnp.concatenate(k_parts[2 * hd:2 * hd + 2], axis=0)
        kb_ref[0, tok, hd * DIFF_V:(hd + 1) * DIFF_V] = k12.T.astype(BF16)
        vb_ref[0, hd * DIFF_VS:hd * DIFF_VS + DIFF_V, tok] = zvb[hd * DIFF_V:(hd + 1) * DIFF_V].astype(BF16)
        vb_ref[0, hd * DIFF_VS + DIFF_V:(hd + 1) * DIFF_VS, tok] = sum_rows
    emit_gates(3)


def _prep(n_t, pos3, w):
    bsz, d, s = n_t.shape
    tm = PREP_TOKEN_TILE
    n_in = w["win_t"].shape[0]
    wa, wb = MLA_HEADS * HEAD_PAD, DIFF_HEADS * DIFF_V
    n_gate = n_in - (MLA_Q_LORA + MLA_KV_LORA + MLA_ROPE + 3 * wb)

    def fm(rows):
        return (jax.ShapeDtypeStruct((bsz, rows, s), BF16),
                pl.BlockSpec((1, rows, tm), lambda b, i: (b, 0, i)))

    def tmaj(cols):
        return (jax.ShapeDtypeStruct((bsz, s, cols), BF16),
                pl.BlockSpec((1, tm, cols), lambda b, i: (b, i, 0)))

    outs = [fm(wa), tmaj(wa), fm(MLA_HEADS * MLA_VS), fm(wb), tmaj(wb), fm(DIFF_HEADS * DIFF_VS), fm(n_gate)]
    consts = [w["win_t"], w["wuq_t"], w["wukv_t"], w["table"]]
    in_specs = [
        pl.BlockSpec((1, d, tm), lambda b, i: (b, 0, i)),
        pl.BlockSpec((1, 1, tm), lambda b, i: (b, 0, i)),
    ] + [_const_spec(a.shape) for a in consts]
    return pl.pallas_call(
        _prep_kernel,
        grid=(bsz, s // tm),
        in_specs=in_specs,
        out_specs=[o[1] for o in outs],
        out_shape=[o[0] for o in outs],
        compiler_params=pltpu.CompilerParams(
            dimension_semantics=("arbitrary", "arbitrary"),
            vmem_limit_bytes=V7X_VMEM_LIMIT),
        name="prep",
    )(n_t, pos3, *consts)


def _normalise(acc, rows):
    return acc[:rows] * (1.0 / acc[rows:rows + 1])


def _attention_pipeline(n_heads, s_len, tq, kc, n_streams, scores_fn, v_fn, finish_fn):
    units = [(hd, q0, c0) for hd in range(n_heads) for q0 in range(0, s_len, tq)
             for c0 in range(0, s_len, kc)]
    streams = range(n_streams)
    n = len(units)
    s_cur = [scores_fn(*units[0], t) for t in streams]
    m = [None] * n_streams
    acc = [None] * n_streams
    pending = None
    for i in range(n + 1):
        s_nxt = [scores_fn(*units[i + 1], t) for t in streams] if i + 1 < n else None
        if pending is not None:
            (hd, q0, c0), alphas, probs = pending
            v = v_fn(hd, c0)
            for t in streams:
                o = jnp.dot(v, probs[t], preferred_element_type=F32)
                acc[t] = o if alphas[t] is None else alphas[t] * acc[t] + o
            if c0 + kc == s_len:
                finish_fn(hd, q0, acc)
                acc = [None] * n_streams
            pending = None
        if i < n:
            _, _, c0 = units[i]
            alphas, probs = [], []
            for t in streams:
                cmax = jnp.max(s_cur[t], axis=0, keepdims=True)
                if c0 == 0:
                    m[t] = cmax
                    alphas.append(None)
                else:
                    m_new = jnp.maximum(m[t], cmax)
                    alphas.append(jnp.exp2(m[t] - m_new))
                    m[t] = m_new
                probs.append(jnp.exp2(s_cur[t] - m[t]).astype(BF16))
            pending = (units[i], alphas, probs)
            s_cur = s_nxt


def _mla_kernel(k_ref, q_ref, v_ref, o_ref):
    def scores(hd, q0, c0, t):
        return jnp.dot(k_ref[0, c0:c0 + MLA_KEY_CHUNK, hd * HEAD_PAD:(hd + 1) * HEAD_PAD],
                       q_ref[0, hd * HEAD_PAD:(hd + 1) * HEAD_PAD, q0:q0 + MLA_Q_TILE],
                       preferred_element_type=F32)

    def values(hd, c0):
        return v_ref[0, hd * MLA_VS:(hd + 1) * MLA_VS, c0:c0 + MLA_KEY_CHUNK]

    def finish(hd, q0, accs):
        o_ref[0, hd * MLA_V:(hd + 1) * MLA_V, q0:q0 + MLA_Q_TILE] = (
            _normalise(accs[0], MLA_V).astype(BF16))

    _attention_pipeline(MLA_HEADS_PER_STEP, k_ref.shape[1], MLA_Q_TILE, MLA_KEY_CHUNK, 1,
                        scores, values, finish)


def _mla_attention(ka, qa, va):
    bsz, s, _ = ka.shape
    hp = MLA_HEADS_PER_STEP
    return pl.pallas_call(
        _mla_kernel,
        grid=(bsz, MLA_HEADS // hp),
        in_specs=[
            pl.BlockSpec((1, s, hp * HEAD_PAD), lambda b, h: (b, 0, h)),
            pl.BlockSpec((1, hp * HEAD_PAD, s), lambda b, h: (b, h, 0)),
            pl.BlockSpec((1, hp * MLA_VS, s), lambda b, h: (b, h, 0)),
        ],
        out_specs=pl.BlockSpec((1, hp * MLA_V, s), lambda b, h: (b, h, 0)),
        out_shape=jax.ShapeDtypeStruct((bsz, MLA_HEADS * MLA_V, s), BF16),
        compiler_params=pltpu.CompilerParams(
            dimension_semantics=("arbitrary", "arbitrary"),
            vmem_limit_bytes=V7X_VMEM_LIMIT),
        name="mla_attn",
    )(ka, qa, va)


def _diff_kernel(k_ref, q_ref, v_ref, lq1_ref, lk1_ref, lq2_ref, lk2_ref, subln_ref, o_ref, *,
                 lambda_init):
    lam = (jnp.exp(jnp.sum(lq1_ref[...] * lk1_ref[...], axis=-1, keepdims=True))
           - jnp.exp(jnp.sum(lq2_ref[...] * lk2_ref[...], axis=-1, keepdims=True))
           + lambda_init)
    subln = _lane_tile(subln_ref[...], DIFF_Q_TILE)
    zeros = jnp.zeros((DIFF_HD, DIFF_Q_TILE), BF16)

    def scores(hd, q0, c0, t):
        lo = hd * DIFF_V
        q12 = q_ref[0, lo:lo + DIFF_V, q0:q0 + DIFF_Q_TILE]
        k12 = k_ref[0, c0:c0 + DIFF_KEY_CHUNK, lo:lo + DIFF_V]
        if t == 0:
            q = jnp.concatenate([q12[:DIFF_HD], zeros], axis=0)
        else:
            q = jnp.concatenate([zeros, q12[DIFF_HD:]], axis=0)
        return jnp.dot(k12, q, preferred_element_type=F32)

    def values(hd, c0):
        return v_ref[0, hd * DIFF_VS:(hd + 1) * DIFF_VS, c0:c0 + DIFF_KEY_CHUNK]

    def finish(hd, q0, accs):
        o = _normalise(accs[0], DIFF_V) - lam * _normalise(accs[1], DIFF_V)
        r = lax.rsqrt(jnp.mean(o * o, axis=0, keepdims=True) + NORM_EPS)
        o_ref[0, hd * DIFF_V:(hd + 1) * DIFF_V, q0:q0 + DIFF_Q_TILE] = (
            (o * r * subln) * (1.0 - lambda_init)).astype(BF16)

    _attention_pipeline(DIFF_HEADS_PER_STEP, k_ref.shape[1], DIFF_Q_TILE, DIFF_KEY_CHUNK, 2,
                        scores, values, finish)


def _diff_attention(kb, qb, vb, lq1, lk1, lq2, lk2, subln, lambda_init):
    bsz, s, _ = kb.shape
    hp = DIFF_HEADS_PER_STEP
    vec = _const_spec((1, DIFF_HD))
    return pl.pallas_call(
        functools.partial(_diff_kernel, lambda_init=lambda_init),
        grid=(bsz, DIFF_HEADS // hp),
        in_specs=[
            pl.BlockSpec((1, s, hp * DIFF_V), lambda b, h: (b, 0, h)),
            pl.BlockSpec((1, hp * DIFF_V, s), lambda b, h: (b, h, 0)),
            pl.BlockSpec((1, hp * DIFF_VS, s), lambda b, h: (b, h, 0)),
            vec, vec, vec, vec,
            _const_spec((DIFF_V, LANES)),
        ],
        out_specs=pl.BlockSpec((1, hp * DIFF_V, s), lambda b, h: (b, h, 0)),
        out_shape=jax.ShapeDtypeStruct((bsz, DIFF_HEADS * DIFF_V, s), BF16),
        compiler_params=pltpu.CompilerParams(
            dimension_semantics=("arbitrary", "arbitrary"),
            vmem_limit_bytes=V7X_VMEM_LIMIT),
        name="diff_attn",
    )(kb, qb, vb, lq1, lk1, lq2, lk2, subln)


def _merge_kernel(oa_ref, ob_ref, g_ref, h_ref, mod_ref, woa_ref, wob_ref, wout_ref, o_ref):
    d = h_ref.shape[2]
    gate = mod_ref[0, 5:6, :]
    groups = [slice(t0, t0 + MERGE_SUB_TILE) for t0 in range(0, h_ref.shape[1], MERGE_SUB_TILE)]

    def branch_mix(tok):
        ya = jnp.dot(woa_ref[...], oa_ref[0, :, tok], preferred_element_type=F32)
        yb = jnp.dot(wob_ref[...], ob_ref[0, :, tok], preferred_element_type=F32)
        ga = g_ref[0, :d, tok].astype(F32)
        gb = g_ref[0, d:, tok].astype(F32)
        return (ga * ya + gb * yb).astype(BF16)

    def project(tok, mix):
        y_t = jnp.dot(wout_ref[...], mix, preferred_element_type=F32)
        o_ref[0, tok, :] = h_ref[0, tok, :] + gate * y_t.T

    mixes = [branch_mix(groups[0])]
    for prev, cur in zip(groups[:-1], groups[1:]):
        mixes.append(branch_mix(cur))
        project(prev, mixes[-2])
    project(groups[-1], mixes[-1])


def _merge(oa, ob, g, h, mod, woa_t, wob_t, wout_t):
    bsz, s, d = h.shape
    tm = MERGE_TOKEN_TILE
    return pl.pallas_call(
        _merge_kernel,
        grid=(bsz, s // tm),
        in_specs=[
            pl.BlockSpec((1, oa.shape[1], tm), lambda b, i: (b, 0, i)),
            pl.BlockSpec((1, ob.shape[1], tm), lambda b, i: (b, 0, i)),
            pl.BlockSpec((1, g.shape[1], tm), lambda b, i: (b, 0, i)),
            pl.BlockSpec((1, tm, d), lambda b, i: (b, i, 0)),
            pl.BlockSpec((1, N_MOD, d), lambda b, i: (b, 0, 0)),
            _const_spec(woa_t.shape),
            _const_spec(wob_t.shape),
            _const_spec(wout_t.shape),
        ],
        out_specs=pl.BlockSpec((1, tm, d), lambda b, i: (b, i, 0)),
        out_shape=jax.ShapeDtypeStruct((bsz, s, d), F32),
        compiler_params=pltpu.CompilerParams(
            dimension_semantics=("arbitrary", "arbitrary"),
            vmem_limit_bytes=V7X_VMEM_LIMIT),
        name="merge",
    )(oa, ob, g, h, mod, woa_t, wob_t, wout_t)


def _lane_bcast(v, rows=None):
    n = v.shape[0]
    out = jnp.broadcast_to(v.astype(F32)[:, None], (n, LANES))
    if rows is not None and rows > n:
        out = jnp.pad(out, ((0, rows - n), (0, 0)))
    return out


def _rope_freqs(half, theta):
    return 1.0 / (theta ** (jnp.arange(half, dtype=F32) / half))


def _prep_table(vectors):
    parts = []
    for name, (_, rows) in PREP_TABLE.items():
        v = vectors[name].astype(F32)
        parts.append(jnp.pad(v, (0, rows - v.shape[0])))
    return jnp.concatenate(parts)


def _layer_weights(l, w_in, mla_q_norm, mla_w_uq, mla_kv_norm, mla_w_ukv, mla_q_gain, mla_k_gain,
                   diff_q_gain, diff_k_gain):
    wuq = mla_w_uq[l].reshape(MLA_Q_LORA, MLA_HEADS, MLA_QK)
    wuq = jnp.pad(wuq, ((0, 0), (0, 0), (0, HEAD_PAD - MLA_QK))).reshape(MLA_Q_LORA, -1)
    return {
        "win_t": w_in[l].T.astype(BF16),
        "wuq_t": wuq.T.astype(BF16),
        "wukv_t": mla_w_ukv[l].T.astype(BF16),
        "table": _lane_bcast(_prep_table({
            "qnorm": mla_q_norm[l], "kvnorm": mla_kv_norm[l],
            "qgain_a": mla_q_gain[l], "kgain_a": mla_k_gain[l],
            "qgain_b": diff_q_gain[l], "kgain_b": diff_k_gain[l],
            "freq_a": _rope_freqs(MLA_ROPE // 2, MLA_THETA),
            "freq_b": _rope_freqs(DIFF_ROT // 2, DIFF_THETA)})),
    }


def kernel(x, c, positions, w_ada, b_ada, ffn1_norm, ffn1_w_gate, ffn1_w_up, ffn1_w_down, mix_norm, w_in, mla_q_norm, mla_w_uq, mla_kv_norm, mla_w_ukv, mla_q_gain, mla_k_gain, mla_w_o, diff_q_gain, diff_k_gain, diff_lambda_q1, diff_lambda_k1, diff_lambda_q2, diff_lambda_k2, diff_subln, diff_w_o, w_out, ffn2_norm, ffn2_w_gate, ffn2_w_up, ffn2_w_down, final_norm):
    bsz, s, d = x.shape
    depth = w_ada.shape[0]
    pos3 = positions.reshape(bsz, 1, s)
    h = x
    for l in range(depth):
        lambda_init = 0.8 - 0.6 * math.exp(-0.3 * l)
        mod = _ada(c, w_ada[l], b_ada[l]).reshape(bsz, N_MOD, d)

        h, n_mix = _ffn(h, mod, ffn1_norm[l], ffn1_w_gate[l], ffn1_w_up[l], ffn1_w_down[l],
                        sub=0, next_gain=mix_norm[l])

        w = _layer_weights(l, w_in, mla_q_norm, mla_w_uq, mla_kv_norm, mla_w_ukv, mla_q_gain,
                           mla_k_gain, diff_q_gain, diff_k_gain)
        qa, ka, va, qb, kb, vb, g = _prep(n_mix, pos3, w)
        oa = _mla_attention(ka, qa, va)
        ob = _diff_attention(kb, qb, vb,
                             diff_lambda_q1[l].reshape(1, -1), diff_lambda_k1[l].reshape(1, -1),
                             diff_lambda_q2[l].reshape(1, -1), diff_lambda_k2[l].reshape(1, -1),
                             _lane_bcast(diff_subln[l]), lambda_init)
        h = _merge(oa, ob, g, h, mod, mla_w_o[l].T.astype(BF16), diff_w_o[l].T.astype(BF16),
                   w_out[l].T.astype(BF16))

        h = _ffn(h, mod, ffn2_norm[l], ffn2_w_gate[l], ffn2_w_up[l], ffn2_w_down[l],
                 sub=2, final_gain=final_norm[l])
    return h
```

```python
import functools
import math

import jax
import jax.numpy as jnp
from jax import lax
from jax.experimental import pallas as pl
from jax.experimental.pallas import tpu as pltpu

F32 = jnp.float32
BF16 = jnp.bfloat16

NORM_EPS = 1e-6
N_MOD = 9

MLA_HEADS = 8
MLA_NOPE = 64
MLA_ROPE = 32
MLA_QK = MLA_NOPE + MLA_ROPE
MLA_V = 64
MLA_Q_LORA = 384
MLA_KV_LORA = 256
MLA_THETA = 10000.0
DIFF_HEADS = 4
DIFF_HD = 64
DIFF_V = 2 * DIFF_HD
DIFF_THETA = 500000.0
DIFF_ROT = DIFF_HD // 4

LANES = 128
HEAD_PAD = 128
SUM_ROWS = 16
MLA_VS = MLA_V + SUM_ROWS
DIFF_VS = DIFF_V + SUM_ROWS
LOG2E = math.log2(math.e)
V7X_VMEM_LIMIT = 56 * 1024 * 1024

def _table_layout(sizes):
    layout, lo = {}, 0
    for name, rows in sizes:
        layout[name] = (lo, rows)
        lo += rows
    return layout


PREP_TABLE = _table_layout([
    ("qnorm", MLA_Q_LORA), ("kvnorm", MLA_KV_LORA), ("qgain_a", HEAD_PAD), ("kgain_a", HEAD_PAD),
    ("qgain_b", DIFF_HD), ("kgain_b", DIFF_HD), ("freq_a", MLA_ROPE // 2), ("freq_b", DIFF_ROT // 2)])

PREP_TOKEN_TILE = 1024
PREP_SUB_TILE = 1024
FFN_TOKEN_TILE = 1024
FFN_SUB_TILE = 512
MERGE_TOKEN_TILE = 1024
MERGE_SUB_TILE = 512
MLA_Q_TILE = 512
DIFF_Q_TILE = 512
MLA_KEY_CHUNK = 256
DIFF_KEY_CHUNK = 512
MLA_HEADS_PER_STEP = 2
DIFF_HEADS_PER_STEP = 2
FF_CHUNK = 256
LOAD_SLOTS = 4
ADA_COL_BLOCK = 2304


def _sigmoid(x):
    return 1.0 / (1.0 + jnp.exp(-x))


def _rms_rows(x, gain):
    ms = jnp.mean(x * x, axis=-1, keepdims=True)
    return x * lax.rsqrt(ms + NORM_EPS) * gain


def _lane_tile(g, width):
    return jnp.tile(g, (1, width // LANES))


def _const_spec(shape):
    return pl.BlockSpec(shape, lambda *_: (0,) * len(shape), pipeline_mode=pl.Buffered(1))


def _ada_kernel(c_ref, w_ref, b_ref, o_ref):
    c = c_ref[...]
    bsz = c.shape[0]
    cond = c * _sigmoid(c)
    c_hi = cond.astype(BF16).astype(F32)
    lhs = jnp.concatenate([c_hi, cond - c_hi], axis=0).astype(BF16)
    w = w_ref[...]
    w_hi = w.astype(BF16)
    w_lo = (w - w_hi.astype(F32)).astype(BF16)
    a = jnp.dot(lhs, w_hi, preferred_element_type=F32)
    b = jnp.dot(lhs, w_lo, preferred_element_type=F32)
    o_ref[...] = a[:bsz] + a[bsz:] + b[:bsz] + b_ref[...]


def _ada(c, w_ada, b_ada):
    bsz, d = c.shape
    cols = w_ada.shape[1]
    return pl.pallas_call(
        _ada_kernel,
        grid=(cols // ADA_COL_BLOCK,),
        in_specs=[
            pl.BlockSpec((bsz, d), lambda j: (0, 0)),
            pl.BlockSpec((d, ADA_COL_BLOCK), lambda j: (0, j)),
            pl.BlockSpec((1, ADA_COL_BLOCK), lambda j: (0, j)),
        ],
        out_specs=pl.BlockSpec((bsz, ADA_COL_BLOCK), lambda j: (0, j)),
        out_shape=jax.ShapeDtypeStruct((bsz, cols), F32),
        compiler_params=pltpu.CompilerParams(dimension_semantics=("arbitrary",),
                                             vmem_limit_bytes=V7X_VMEM_LIMIT),
        name="ada",
    )(c, w_ada, b_ada.reshape(1, cols))


def _adaln(x, gain, mod_ref, sub, dtype=BF16):
    shift = mod_ref[0, 3 * sub:3 * sub + 1, :]
    scale = mod_ref[0, 3 * sub + 1:3 * sub + 2, :]
    return (_rms_rows(x, gain) * (1.0 + scale) + shift).astype(dtype)


class _WeightStream:
    def __init__(self, wg_hbm, wu_hbm, wd_hbm, wgu_s, wd_s, stage_c, stage_r, sem):
        self.sem = sem
        self.jobs = []
        for c0 in range(0, wd_s.shape[0], FF_CHUNK):
            blk = slice(c0, c0 + FF_CHUNK)
            self.jobs.append((wg_hbm.at[:, blk], stage_c, wgu_s.at[:, 2 * c0:2 * c0 + FF_CHUNK]))
            self.jobs.append((wu_hbm.at[:, blk], stage_c,
                              wgu_s.at[:, 2 * c0 + FF_CHUNK:2 * (c0 + FF_CHUNK)]))
            self.jobs.append((wd_hbm.at[blk, :], stage_r, wd_s.at[blk, :]))
        self.landed = 0
        for i in range(min(LOAD_SLOTS - 1, len(self.jobs))):
            self._copy(i).start()

    def _copy(self, i):
        src, stage, _ = self.jobs[i]
        return pltpu.make_async_copy(src, stage.at[i % LOAD_SLOTS], self.sem.at[i % LOAD_SLOTS])

    def land(self, count):
        for i in range(self.landed, self.landed + count):
            if i + LOAD_SLOTS - 1 < len(self.jobs):
                self._copy(i + LOAD_SLOTS - 1).start()
            self._copy(i).wait()
            _, stage, dst = self.jobs[i]
            dst[...] = stage[i % LOAD_SLOTS].astype(BF16)
        self.landed += count


def _ffn_kernel(*refs, sub, emit_next, final):
    refs = list(refs)
    x_ref, mod_ref, gain_ref = refs.pop(0), refs.pop(0), refs.pop(0)
    w_hbm = [refs.pop(0) for _ in range(3)]
    next_gain_ref = refs.pop(0) if emit_next else None
    fgain_ref = refs.pop(0) if final else None
    o_ref = refs.pop(0)
    n_next_ref = refs.pop(0) if emit_next else None
    a_ref, wgu_ref, wd_ref, stage_c, stage_r, sem = refs

    d_ff = wd_ref.shape[0]
    gate = mod_ref[0, 3 * sub + 2:3 * sub + 3, :]
    halves = [slice(r0, r0 + FFN_SUB_TILE) for r0 in range(0, x_ref.shape[1], FFN_SUB_TILE)]
    blocks = range(0, d_ff, FF_CHUNK)
    first = (pl.program_id(0) == 0) & (pl.program_id(1) == 0)

    def norm(rows):
        return _adaln(x_ref[0, rows, :], gain_ref[...], mod_ref, sub)

    def up_block(n, rows, c0):
        gu = jnp.dot(n, wgu_ref[:, 2 * c0:2 * (c0 + FF_CHUNK)], preferred_element_type=F32)
        g, u = gu[:, :FF_CHUNK], gu[:, FF_CHUNK:]
        a_ref[rows, c0:c0 + FF_CHUNK] = ((g * _sigmoid(g)) * u).astype(BF16)

    def up(rows):
        n = norm(rows)
        for c0 in blocks:
            up_block(n, rows, c0)

    def down(rows):
        f = jnp.dot(a_ref[rows, :], wd_ref[...], preferred_element_type=F32)
        h = x_ref[0, rows, :] + (0.5 * gate) * f
        if final:
            h = _rms_rows(h, fgain_ref[...])
        o_ref[0, rows, :] = h
        if emit_next:
            n_next_ref[0, :, rows] = _adaln(h, next_gain_ref[...], mod_ref, sub + 1, F32).T.astype(BF16)

    @pl.when(first)
    def _():
        stream = _WeightStream(*w_hbm, wgu_ref, wd_ref, stage_c, stage_r, sem)
        norms = [norm(rows) for rows in halves]
        for c0 in blocks:
            stream.land(3)
            for n, rows in zip(norms, halves):
                up_block(n, rows, c0)
        for rows in halves:
            down(rows)

    @pl.when(jnp.logical_not(first))
    def _():
        up(halves[0])
        for prev, cur in zip(halves[:-1], halves[1:]):
            up(cur)
            down(prev)
        down(halves[-1])


def _ffn(h, mod, gain, wg, wu, wd, *, sub, next_gain=None, final_gain=None):
    bsz, s, d = h.shape
    d_ff = wg.shape[1]
    tm = FFN_TOKEN_TILE
    tile = pl.BlockSpec((1, tm, d), lambda b, i: (b, i, 0))
    hbm = pl.BlockSpec(memory_space=pl.ANY)
    in_specs = [tile, pl.BlockSpec((1, N_MOD, d), lambda b, i: (b, 0, 0)), _const_spec((1, d)),
                hbm, hbm, hbm]
    args = [h, mod, gain.reshape(1, d), wg, wu, wd]
    for extra in (next_gain, final_gain):
        if extra is not None:
            in_specs.append(_const_spec((1, d)))
            args.append(extra.reshape(1, d))
    out_specs, out_shape = [tile], [jax.ShapeDtypeStruct((bsz, s, d), F32)]
    if next_gain is not None:
        out_specs.append(pl.BlockSpec((1, d, tm), lambda b, i: (b, 0, i)))
        out_shape.append(jax.ShapeDtypeStruct((bsz, d, s), BF16))
    outs = pl.pallas_call(
        functools.partial(_ffn_kernel, sub=sub, emit_next=next_gain is not None,
                          final=final_gain is not None),
        grid=(bsz, s // tm),
        in_specs=in_specs,
        out_specs=out_specs,
        out_shape=out_shape,
        scratch_shapes=[
            pltpu.VMEM((tm, d_ff), BF16),
            pltpu.VMEM((d, 2 * d_ff), BF16), pltpu.VMEM((d_ff, d), BF16),
            pltpu.VMEM((LOAD_SLOTS, d, FF_CHUNK), F32), pltpu.VMEM((LOAD_SLOTS, FF_CHUNK, d), F32),
            pltpu.SemaphoreType.DMA((LOAD_SLOTS,)),
        ],
        compiler_params=pltpu.CompilerParams(
            dimension_semantics=("arbitrary", "arbitrary"),
            vmem_limit_bytes=V7X_VMEM_LIMIT),
        name="ffn%d" % sub,
    )(*args)
    return outs if next_gain is not None else outs[0]


def _rope_rows(x1, x2, cos, sin):
    return x1 * cos - x2 * sin, x2 * cos + x1 * sin


def _prep_kernel(*refs):
    for t0 in range(0, refs[0].shape[2], PREP_SUB_TILE):
        _prep_group(slice(t0, t0 + PREP_SUB_TILE), *refs)


def _prep_group(tok, n_ref, pos_ref, win_ref, wuq_ref, wukv_ref,
                tab_ref,
                qa_ref, ka_ref, va_ref, qb_ref, kb_ref, vb_ref, g_ref):
    tm = PREP_SUB_TILE
    n_t = n_ref[0, :, tok]

    def row_consts(name):
        lo, rows = PREP_TABLE[name]
        return _lane_tile(tab_ref[lo:lo + rows, :], tm)

    def proj_t(r0, r1):
        return jnp.dot(win_ref[r0:r1, :], n_t, preferred_element_type=F32)

    o_q, o_kv, o_kr = 0, MLA_Q_LORA, MLA_Q_LORA + MLA_KV_LORA
    o_qb = o_kr + MLA_ROPE
    w_b = DIFF_HEADS * DIFF_V
    o_kb, o_vb, o_g = o_qb + w_b, o_qb + 2 * w_b, o_qb + 3 * w_b
    g_chunk = g_ref.shape[1] // 4

    def emit_gates(i):
        r0 = i * g_chunk
        z = proj_t(o_g + r0, o_g + r0 + g_chunk)
        g_ref[0, r0:r0 + g_chunk, tok] = _sigmoid(z).astype(BF16)

    z_a = proj_t(o_q, o_qb)
    emit_gates(0)
    zq, zkv, kr = z_a[o_q:o_kv], z_a[o_kv:o_kr], z_a[o_kr:o_qb]
    rq = lax.rsqrt(jnp.mean(zq * zq, axis=0, keepdims=True) + NORM_EPS)
    zqn = (zq * rq * row_consts("qnorm")).astype(BF16)
    q_all = jnp.dot(wuq_ref[...], zqn, preferred_element_type=F32)
    rkv = lax.rsqrt(jnp.mean(zkv * zkv, axis=0, keepdims=True) + NORM_EPS)
    zkvn = (zkv * rkv * row_consts("kvnorm")).astype(BF16)
    kv_all = jnp.dot(wukv_ref[...], zkvn, preferred_element_type=F32)
    emit_gates(1)

    pos = pos_ref[0, :, tok].astype(F32)
    ang_a = pos * row_consts("freq_a")
    cos_a, sin_a = jnp.cos(ang_a), jnp.sin(ang_a)
    ang_b = pos * row_consts("freq_b")
    cos_b, sin_b = jnp.cos(ang_b), jnp.sin(ang_b)

    qgain_a = row_consts("qgain_a")
    kgain_a = row_consts("kgain_a")
    q_scale = LOG2E / math.sqrt(MLA_QK)
    pad_rows = jnp.zeros((HEAD_PAD - MLA_QK, tm), F32)
    sum_rows = (lax.broadcasted_iota(jnp.int32, (SUM_ROWS, tm), 0) == 0).astype(BF16)
    half = MLA_ROPE // 2

    def head_norm_rope_a(xh, gain):
        r = lax.rsqrt(jnp.sum(xh * xh, axis=0, keepdims=True) * (1.0 / MLA_QK) + NORM_EPS)
        xh = xh * r * gain
        r1, r2 = _rope_rows(xh[MLA_NOPE:MLA_NOPE + half], xh[MLA_NOPE + half:MLA_QK], cos_a, sin_a)
        return jnp.concatenate([xh[:MLA_NOPE], r1, r2, xh[MLA_QK:]], axis=0)

    zqb = proj_t(o_qb, o_kb)
    zkb = proj_t(o_kb, o_vb)
    for hd in range(MLA_HEADS):
        lo = hd * HEAD_PAD
        qh = head_norm_rope_a(q_all[lo:lo + HEAD_PAD], qgain_a) * q_scale
        qa_ref[0, lo:lo + HEAD_PAD, tok] = qh.astype(BF16)
        kvh = kv_all[lo:lo + HEAD_PAD]
        va_ref[0, hd * MLA_VS:hd * MLA_VS + MLA_V, tok] = kvh[MLA_NOPE:].astype(BF16)
        va_ref[0, hd * MLA_VS + MLA_V:(hd + 1) * MLA_VS, tok] = sum_rows
        kh = jnp.concatenate([kvh[:MLA_NOPE], kr, pad_rows], axis=0)
        kh = head_norm_rope_a(kh, kgain_a)
        ka_ref[0, tok, lo:lo + HEAD_PAD] = kh.T.astype(BF16)
    emit_gates(2)

    qgain_b = row_consts("qgain_b")
    kgain_b = row_consts("kgain_b")
    qb_scale = LOG2E / math.sqrt(DIFF_HD)
    hb = DIFF_ROT // 2

    def head_norm_rope_b(xh, gain):
        r = lax.rsqrt(jnp.mean(xh * xh, axis=0, keepdims=True) + NORM_EPS)
        xh = xh * r * gain
        r1, r2 = _rope_rows(xh[:hb], xh[hb:DIFF_ROT], cos_b, sin_b)
        return jnp.concatenate([r1, r2, xh[DIFF_ROT:]], axis=0)

    zvb = proj_t(o_vb, o_g)
    k_parts = []
    for blk in range(2 * DIFF_HEADS):
        lo = blk * DIFF_HD
        qh = head_norm_rope_b(zqb[lo:lo + DIFF_HD], qgain_b) * qb_scale
        qb_ref[0, lo:lo + DIFF_HD, tok] = qh.astype(BF16)
        k_parts.append(head_norm_rope_b(zkb[lo:lo + DIFF_HD], kgain_b))
    for hd in range(DIFF_HEADS):
        k12 = jnp.concatenate(k_parts[2 * hd:2 * hd + 2], axis=0)
        kb_ref[0, tok, hd * DIFF_V:(hd + 1) * DIFF_V] = k12.T.astype(BF16)
        vb_ref[0, hd * DIFF_VS:hd * DIFF_VS + DIFF_V, tok] = zvb[hd * DIFF_V:(hd + 1) * DIFF_V].astype(BF16)
        vb_ref[0, hd * DIFF_VS + DIFF_V:(hd + 1) * DIFF_VS, tok] = sum_rows
    emit_gates(3)


def _prep(n_t, pos3, w):
    bsz, d, s = n_t.shape
    tm = PREP_TOKEN_TILE
    n_in = w["win_t"].shape[0]
    wa, wb = MLA_HEADS * HEAD_PAD, DIFF_HEADS * DIFF_V
    n_gate = n_in - (MLA_Q_LORA + MLA_KV_LORA + MLA_ROPE + 3 * wb)

    def fm(rows):
        return (jax.ShapeDtypeStruct((bsz, rows, s), BF16),
                pl.BlockSpec((1, rows, tm), lambda b, i: (b, 0, i)))

    def tmaj(cols):
        return (jax.ShapeDtypeStruct((bsz, s, cols), BF16),
                pl.BlockSpec((1, tm, cols), lambda b, i: (b, i, 0)))

    outs = [fm(wa), tmaj(wa), fm(MLA_HEADS * MLA_VS), fm(wb), tmaj(wb), fm(DIFF_HEADS * DIFF_VS), fm(n_gate)]
    consts = [w["win_t"], w["wuq_t"], w["wukv_t"], w["table"]]
    in_specs = [
        pl.BlockSpec((1, d, tm), lambda b, i: (b, 0, i)),
        pl.BlockSpec((1, 1, tm), lambda b, i: (b, 0, i)),
    ] + [_const_spec(a.shape) for a in consts]
    return pl.pallas_call(
        _prep_kernel,
        grid=(bsz, s // tm),
        in_specs=in_specs,
        out_specs=[o[1] for o in outs],
        out_shape=[o[0] for o in outs],
        compiler_params=pltpu.CompilerParams(
            dimension_semantics=("arbitrary", "arbitrary"),
            vmem_limit_bytes=V7X_VMEM_LIMIT),
        name="prep",
    )(n_t, pos3, *consts)


def _normalise(acc, rows):
    return acc[:rows] * (1.0 / acc[rows:rows + 1])


def _attention_pipeline(n_heads, s_len, tq, kc, n_streams, scores_fn, v_fn, finish_fn):
    units = [(hd, q0, c0) for hd in range(n_heads) for q0 in range(0, s_len, tq)
             for c0 in range(0, s_len, kc)]
    streams = range(n_streams)
    n = len(units)
    s_cur = [scores_fn(*units[0], t) for t in streams]
    m = [None] * n_streams
    acc = [None] * n_streams
    pending = None
    for i in range(n + 1):
        s_nxt = [scores_fn(*units[i + 1], t) for t in streams] if i + 1 < n else None
        if pending is not None:
            (hd, q0, c0), alphas, probs = pending
            v = v_fn(hd, c0)
            for t in streams:
                o = jnp.dot(v, probs[t], preferred_element_type=F32)
                acc[t] = o if alphas[t] is None else alphas[t] * acc[t] + o
            if c0 + kc == s_len:
                finish_fn(hd, q0, acc)
                acc = [None] * n_streams
            pending = None
        if i < n:
            _, _, c0 = units[i]
            alphas, probs = [], []
            for t in streams:
                cmax = jnp.max(s_cur[t], axis=0, keepdims=True)
                if c0 == 0:
                    m[t] = cmax
                    alphas.append(None)
                else:
                    m_new = jnp.maximum(m[t], cmax)
                    alphas.append(jnp.exp2(m[t] - m_new))
                    m[t] = m_new
                probs.append(jnp.exp2(s_cur[t] - m[t]).astype(BF16))
            pending = (units[i], alphas, probs)
            s_cur = s_nxt


def _mla_kernel(k_ref, q_ref, v_ref, o_ref):
    def scores(hd, q0, c0, t):
        return jnp.dot(k_ref[0, c0:c0 + MLA_KEY_CHUNK, hd * HEAD_PAD:(hd + 1) * HEAD_PAD],
                       q_ref[0, hd * HEAD_PAD:(hd + 1) * HEAD_PAD, q0:q0 + MLA_Q_TILE],
                       preferred_element_type=F32)

    def values(hd, c0):
        return v_ref[0, hd * MLA_VS:(hd + 1) * MLA_VS, c0:c0 + MLA_KEY_CHUNK]

    def finish(hd, q0, accs):
        o_ref[0, hd * MLA_V:(hd + 1) * MLA_V, q0:q0 + MLA_Q_TILE] = (
            _normalise(accs[0], MLA_V).astype(BF16))

    _attention_pipeline(MLA_HEADS_PER_STEP, k_ref.shape[1], MLA_Q_TILE, MLA_KEY_CHUNK, 1,
                        scores, values, finish)


def _mla_attention(ka, qa, va):
    bsz, s, _ = ka.shape
    hp = MLA_HEADS_PER_STEP
    return pl.pallas_call(
        _mla_kernel,
        grid=(bsz, MLA_HEADS // hp),
        in_specs=[
            pl.BlockSpec((1, s, hp * HEAD_PAD), lambda b, h: (b, 0, h)),
            pl.BlockSpec((1, hp * HEAD_PAD, s), lambda b, h: (b, h, 0)),
            pl.BlockSpec((1, hp * MLA_VS, s), lambda b, h: (b, h, 0)),
        ],
        out_specs=pl.BlockSpec((1, hp * MLA_V, s), lambda b, h: (b, h, 0)),
        out_shape=jax.ShapeDtypeStruct((bsz, MLA_HEADS * MLA_V, s), BF16),
        compiler_params=pltpu.CompilerParams(
            dimension_semantics=("arbitrary", "arbitrary"),
            vmem_limit_bytes=V7X_VMEM_LIMIT),
        name="mla_attn",
    )(ka, qa, va)


def _diff_kernel(k_ref, q_ref, v_ref, lq1_ref, lk1_ref, lq2_ref, lk2_ref, subln_ref, o_ref, *,
                 lambda_init):
    lam = (jnp.exp(jnp.sum(lq1_ref[...] * lk1_ref[...], axis=-1, keepdims=True))
           - jnp.exp(jnp.sum(lq2_ref[...] * lk2_ref[...], axis=-1, keepdims=True))
           + lambda_init)
    subln = _lane_tile(subln_ref[...], DIFF_Q_TILE)
    zeros = jnp.zeros((DIFF_HD, DIFF_Q_TILE), BF16)

    def scores(hd, q0, c0, t):
        lo = hd * DIFF_V
        q12 = q_ref[0, lo:lo + DIFF_V, q0:q0 + DIFF_Q_TILE]
        k12 = k_ref[0, c0:c0 + DIFF_KEY_CHUNK, lo:lo + DIFF_V]
        if t == 0:
            q = jnp.concatenate([q12[:DIFF_HD], zeros], axis=0)
        else:
            q = jnp.concatenate([zeros, q12[DIFF_HD:]], axis=0)
        return jnp.dot(k12, q, preferred_element_type=F32)

    def values(hd, c0):
        return v_ref[0, hd * DIFF_VS:(hd + 1) * DIFF_VS, c0:c0 + DIFF_KEY_CHUNK]

    def finish(hd, q0, accs):
        o = _normalise(accs[0], DIFF_V) - lam * _normalise(accs[1], DIFF_V)
        r = lax.rsqrt(jnp.mean(o * o, axis=0, keepdims=True) + NORM_EPS)
        o_ref[0, hd * DIFF_V:(hd + 1) * DIFF_V, q0:q0 + DIFF_Q_TILE] = (
            (o * r * subln) * (1.0 - lambda_init)).astype(BF16)

    _attention_pipeline(DIFF_HEADS_PER_STEP, k_ref.shape[1], DIFF_Q_TILE, DIFF_KEY_CHUNK, 2,
                        scores, values, finish)


def _diff_attention(kb, qb, vb, lq1, lk1, lq2, lk2, subln, lambda_init):
    bsz, s, _ = kb.shape
    hp = DIFF_HEADS_PER_STEP
    vec = _const_spec((1, DIFF_HD))
    return pl.pallas_call(
        functools.partial(_diff_kernel, lambda_init=lambda_init),
        grid=(bsz, DIFF_HEADS // hp),
        in_specs=[
            pl.BlockSpec((1, s, hp * DIFF_V), lambda b, h: (b, 0, h)),
            pl.BlockSpec((1, hp * DIFF_V, s), lambda b, h: (b, h, 0)),
            pl.BlockSpec((1, hp * DIFF_VS, s), lambda b, h: (b, h, 0)),
            vec, vec, vec, vec,
            _const_spec((DIFF_V, LANES)),
        ],
        out_specs=pl.BlockSpec((1, hp * DIFF_V, s), lambda b, h: (b, h, 0)),
        out_shape=jax.ShapeDtypeStruct((bsz, DIFF_HEADS * DIFF_V, s), BF16),
        compiler_params=pltpu.CompilerParams(
            dimension_semantics=("arbitrary", "arbitrary"),
            vmem_limit_bytes=V7X_VMEM_LIMIT),
        name="diff_attn",
    )(kb, qb, vb, lq1, lk1, lq2, lk2, subln)


def _merge_kernel(oa_ref, ob_ref, g_ref, h_ref, mod_ref, woa_ref, wob_ref, wout_ref, o_ref):
    d = h_ref.shape[2]
    gate = mod_ref[0, 5:6, :]
    groups = [slice(t0, t0 + MERGE_SUB_TILE) for t0 in range(0, h_ref.shape[1], MERGE_SUB_TILE)]

    def branch_mix(tok):
        ya = jnp.dot(woa_ref[...], oa_ref[0, :, tok], preferred_element_type=F32)
        yb = jnp.dot(wob_ref[...], ob_ref[0, :, tok], preferred_element_type=F32)
        ga = g_ref[0, :d, tok].astype(F32)
        gb = g_ref[0, d:, tok].astype(F32)
        return (ga * ya + gb * yb).astype(BF16)

    def project(tok, mix):
        y_t = jnp.dot(wout_ref[...], mix, preferred_element_type=F32)
        o_ref[0, tok, :] = h_ref[0, tok, :] + gate * y_t.T

    mixes = [branch_mix(groups[0])]
    for prev, cur in zip(groups[:-1], groups[1:]):
        mixes.append(branch_mix(cur))
        project(prev, mixes[-2])
    project(groups[-1], mixes[-1])


def _merge(oa, ob, g, h, mod, woa_t, wob_t, wout_t):
    bsz, s, d = h.shape
    tm = MERGE_TOKEN_TILE
    return pl.pallas_call(
        _merge_kernel,
        grid=(bsz, s // tm),
        in_specs=[
            pl.BlockSpec((1, oa.shape[1], tm), lambda b, i: (b, 0, i)),
            pl.BlockSpec((1, ob.shape[1], tm), lambda b, i: (b, 0, i)),
            pl.BlockSpec((1, g.shape[1], tm), lambda b, i: (b, 0, i)),
            pl.BlockSpec((1, tm, d), lambda b, i: (b, i, 0)),
            pl.BlockSpec((1, N_MOD, d), lambda b, i: (b, 0, 0)),
            _const_spec(woa_t.shape),
            _const_spec(wob_t.shape),
            _const_spec(wout_t.shape),
        ],
        out_specs=pl.BlockSpec((1, tm, d), lambda b, i: (b, i, 0)),
        out_shape=jax.ShapeDtypeStruct((bsz, s, d), F32),
        compiler_params=pltpu.CompilerParams(
            dimension_semantics=("arbitrary", "arbitrary"),
            vmem_limit_bytes=V7X_VMEM_LIMIT),
        name="merge",
    )(oa, ob, g, h, mod, woa_t, wob_t, wout_t)


def _lane_bcast(v, rows=None):
    n = v.shape[0]
    out = jnp.broadcast_to(v.astype(F32)[:, None], (n, LANES))
    if rows is not None and rows > n:
        out = jnp.pad(out, ((0, rows - n), (0, 0)))
    return out


def _rope_freqs(half, theta):
    return 1.0 / (theta ** (jnp.arange(half, dtype=F32) / half))


def _prep_table(vectors):
    parts = []
    for name, (_, rows) in PREP_TABLE.items():
        v = vectors[name].astype(F32)
        parts.append(jnp.pad(v, (0, rows - v.shape[0])))
    return jnp.concatenate(parts)


def _layer_weights(l, w_in, mla_q_norm, mla_w_uq, mla_kv_norm, mla_w_ukv, mla_q_gain, mla_k_gain,
                   diff_q_gain, diff_k_gain):
    wuq = mla_w_uq[l].reshape(MLA_Q_LORA, MLA_HEADS, MLA_QK)
    wuq = jnp.pad(wuq, ((0, 0), (0, 0), (0, HEAD_PAD - MLA_QK))).reshape(MLA_Q_LORA, -1)
    return {
        "win_t": w_in[l].T.astype(BF16),
        "wuq_t": wuq.T.astype(BF16),
        "wukv_t": mla_w_ukv[l].T.astype(BF16),
        "table": _lane_bcast(_prep_table({
            "qnorm": mla_q_norm[l], "kvnorm": mla_kv_norm[l],
            "qgain_a": mla_q_gain[l], "kgain_a": mla_k_gain[l],
            "qgain_b": diff_q_gain[l], "kgain_b": diff_k_gain[l],
            "freq_a": _rope_freqs(MLA_ROPE // 2, MLA_THETA),
            "freq_b": _rope_freqs(DIFF_ROT // 2, DIFF_THETA)})),
    }


def kernel(x, c, positions, w_ada, b_ada, ffn1_norm, ffn1_w_gate, ffn1_w_up, ffn1_w_down, mix_norm, w_in, mla_q_norm, mla_w_uq, mla_kv_norm, mla_w_ukv, mla_q_gain, mla_k_gain, mla_w_o, diff_q_gain, diff_k_gain, diff_lambda_q1, diff_lambda_k1, diff_lambda_q2, diff_lambda_k2, diff_subln, diff_w_o, w_out, ffn2_norm, ffn2_w_gate, ffn2_w_up, ffn2_w_down, final_norm):
    bsz, s, d = x.shape
    depth = w_ada.shape[0]
    pos3 = positions.reshape(bsz, 1, s)
    h = x
    for l in range(depth):
        lambda_init = 0.8 - 0.6 * math.exp(-0.3 * l)
        mod = _ada(c, w_ada[l], b_ada[l]).reshape(bsz, N_MOD, d)

        h, n_mix = _ffn(h, mod, ffn1_norm[l], ffn1_w_gate[l], ffn1_w_up[l], ffn1_w_down[l],
                        sub=0, next_gain=mix_norm[l])

        w = _layer_weights(l, w_in, mla_q_norm, mla_w_uq, mla_kv_norm, mla_w_ukv, mla_q_gain,
                           mla_k_gain, diff_q_gain, diff_k_gain)
        qa, ka, va, qb, kb, vb, g = _prep(n_mix, pos3, w)
        oa = _mla_attention(ka, qa, va)
        ob = _diff_attention(kb, qb, vb,
                             diff_lambda_q1[l].reshape(1, -1), diff_lambda_k1[l].reshape(1, -1),
                             diff_lambda_q2[l].reshape(1, -1), diff_lambda_k2[l].reshape(1, -1),
                             _lane_bcast(diff_subln[l]), lambda_init)
        h = _merge(oa, ob, g, h, mod, mla_w_o[l].T.astype(BF16), diff_w_o[l].T.astype(BF16),
                   w_out[l].T.astype(BF16))

        h = _ffn(h, mod, ffn2_norm[l], ffn2_w_gate[l], ffn2_w_up[l], ffn2_w_down[l],
                 sub=2, final_gain=final_norm[l])
    return h
```

```python
import functools
import math

import jax
import jax.numpy as jnp
from jax import lax
from jax.experimental import pallas as pl
from jax.experimental.pallas import tpu as pltpu

F32 = jnp.float32
BF16 = jnp.bfloat16

NORM_EPS = 1e-6
N_MOD = 9

MLA_HEADS = 8
MLA_NOPE = 64
MLA_ROPE = 32
MLA_QK = MLA_NOPE + MLA_ROPE
MLA_V = 64
MLA_Q_LORA = 384
MLA_KV_LORA = 256
MLA_THETA = 10000.0
DIFF_HEADS = 4
DIFF_HD = 64
DIFF_V = 2 * DIFF_HD
DIFF_THETA = 500000.0
DIFF_ROT = DIFF_HD // 4

LANES = 128
HEAD_PAD = 128
SUM_ROWS = 16
MLA_VS = MLA_V + SUM_ROWS
DIFF_VS = DIFF_V + SUM_ROWS
LOG2E = math.log2(math.e)
V7X_VMEM_LIMIT = 56 * 1024 * 1024

def _table_layout(sizes):
    layout, lo = {}, 0
    for name, rows in sizes:
        layout[name] = (lo, rows)
        lo += rows
    return layout


PREP_TABLE = _table_layout([
    ("qnorm", MLA_Q_LORA), ("kvnorm", MLA_KV_LORA), ("qgain_a", HEAD_PAD), ("kgain_a", HEAD_PAD),
    ("qgain_b", DIFF_HD), ("kgain_b", DIFF_HD), ("freq_a", MLA_ROPE // 2), ("freq_b", DIFF_ROT // 2)])

PREP_TOKEN_TILE = 1024
PREP_SUB_TILE = 1024
FFN_TOKEN_TILE = 1024
FFN_SUB_TILE = 512
MERGE_TOKEN_TILE = 1024
MERGE_SUB_TILE = 512
MLA_Q_TILE = 512
DIFF_Q_TILE = 512
MLA_KEY_CHUNK = 256
DIFF_KEY_CHUNK = 512
MLA_HEADS_PER_STEP = 8
DIFF_HEADS_PER_STEP = 4
FF_CHUNK = 256
LOAD_SLOTS = 4
ADA_COL_BLOCK = 2304


def _sigmoid(x):
    return 1.0 / (1.0 + jnp.exp(-x))


def _rms_rows(x, gain):
    ms = jnp.mean(x * x, axis=-1, keepdims=True)
    return x * lax.rsqrt(ms + NORM_EPS) * gain


def _lane_tile(g, width):
    return jnp.tile(g, (1, width // LANES))


def _const_spec(shape):
    return pl.BlockSpec(shape, lambda *_: (0,) * len(shape), pipeline_mode=pl.Buffered(1))


def _ada_kernel(c_ref, w_ref, b_ref, o_ref):
    c = c_ref[...]
    bsz = c.shape[0]
    cond = c * _sigmoid(c)
    c_hi = cond.astype(BF16).astype(F32)
    lhs = jnp.concatenate([c_hi, cond - c_hi], axis=0).astype(BF16)
    w = w_ref[...]
    w_hi = w.astype(BF16)
    w_lo = (w - w_hi.astype(F32)).astype(BF16)
    a = jnp.dot(lhs, w_hi, preferred_element_type=F32)
    b = jnp.dot(lhs, w_lo, preferred_element_type=F32)
    o_ref[...] = a[:bsz] + a[bsz:] + b[:bsz] + b_ref[...]


def _ada(c, w_ada, b_ada):
    bsz, d = c.shape
    cols = w_ada.shape[1]
    return pl.pallas_call(
        _ada_kernel,
        grid=(cols // ADA_COL_BLOCK,),
        in_specs=[
            pl.BlockSpec((bsz, d), lambda j: (0, 0)),
            pl.BlockSpec((d, ADA_COL_BLOCK), lambda j: (0, j)),
            pl.BlockSpec((1, ADA_COL_BLOCK), lambda j: (0, j)),
        ],
        out_specs=pl.BlockSpec((bsz, ADA_COL_BLOCK), lambda j: (0, j)),
        out_shape=jax.ShapeDtypeStruct((bsz, cols), F32),
        compiler_params=pltpu.CompilerParams(dimension_semantics=("arbitrary",),
                                             vmem_limit_bytes=V7X_VMEM_LIMIT),
        name="ada",
    )(c, w_ada, b_ada.reshape(1, cols))


def _adaln(x, gain, mod_ref, sub, dtype=BF16):
    shift = mod_ref[0, 3 * sub:3 * sub + 1, :]
    scale = mod_ref[0, 3 * sub + 1:3 * sub + 2, :]
    return (_rms_rows(x, gain) * (1.0 + scale) + shift).astype(dtype)


class _WeightStream:
    def __init__(self, wg_hbm, wu_hbm, wd_hbm, wgu_s, wd_s, stage_c, stage_r, sem):
        self.sem = sem
        self.jobs = []
        for c0 in range(0, wd_s.shape[0], FF_CHUNK):
            blk = slice(c0, c0 + FF_CHUNK)
            self.jobs.append((wg_hbm.at[:, blk], stage_c, wgu_s.at[:, 2 * c0:2 * c0 + FF_CHUNK]))
            self.jobs.append((wu_hbm.at[:, blk], stage_c,
                              wgu_s.at[:, 2 * c0 + FF_CHUNK:2 * (c0 + FF_CHUNK)]))
            self.jobs.append((wd_hbm.at[blk, :], stage_r, wd_s.at[blk, :]))
        self.landed = 0
        for i in range(min(LOAD_SLOTS - 1, len(self.jobs))):
            self._copy(i).start()

    def _copy(self, i):
        src, stage, _ = self.jobs[i]
        return pltpu.make_async_copy(src, stage.at[i % LOAD_SLOTS], self.sem.at[i % LOAD_SLOTS])

    def land(self, count):
        for i in range(self.landed, self.landed + count):
            if i + LOAD_SLOTS - 1 < len(self.jobs):
                self._copy(i + LOAD_SLOTS - 1).start()
            self._copy(i).wait()
            _, stage, dst = self.jobs[i]
            dst[...] = stage[i % LOAD_SLOTS].astype(BF16)
        self.landed += count


def _ffn_kernel(*refs, sub, emit_next, final):
    refs = list(refs)
    x_ref, mod_ref, gain_ref = refs.pop(0), refs.pop(0), refs.pop(0)
    w_hbm = [refs.pop(0) for _ in range(3)]
    next_gain_ref = refs.pop(0) if emit_next else None
    fgain_ref = refs.pop(0) if final else None
    o_ref = refs.pop(0)
    n_next_ref = refs.pop(0) if emit_next else None
    a_ref, wgu_ref, wd_ref, stage_c, stage_r, sem = refs

    d_ff = wd_ref.shape[0]
    gate = mod_ref[0, 3 * sub + 2:3 * sub + 3, :]
    halves = [slice(r0, r0 + FFN_SUB_TILE) for r0 in range(0, x_ref.shape[1], FFN_SUB_TILE)]
    blocks = range(0, d_ff, FF_CHUNK)
    first = (pl.program_id(0) == 0) & (pl.program_id(1) == 0)

    def norm(rows):
        return _adaln(x_ref[0, rows, :], gain_ref[...], mod_ref, sub)

    def up_block(n, rows, c0):
        gu = jnp.dot(n, wgu_ref[:, 2 * c0:2 * (c0 + FF_CHUNK)], preferred_element_type=F32)
        g, u = gu[:, :FF_CHUNK], gu[:, FF_CHUNK:]
        a_ref[rows, c0:c0 + FF_CHUNK] = ((g * _sigmoid(g)) * u).astype(BF16)

    def up(rows):
        n = norm(rows)
        for c0 in blocks:
            up_block(n, rows, c0)

    def down(rows):
        f = jnp.dot(a_ref[rows, :], wd_ref[...], preferred_element_type=F32)
        h = x_ref[0, rows, :] + (0.5 * gate) * f
        if final:
            h = _rms_rows(h, fgain_ref[...])
        o_ref[0, rows, :] = h
        if emit_next:
            n_next_ref[0, :, rows] = _adaln(h, next_gain_ref[...], mod_ref, sub + 1, F32).T.astype(BF16)

    @pl.when(first)
    def _():
        stream = _WeightStream(*w_hbm, wgu_ref, wd_ref, stage_c, stage_r, sem)
        norms = [norm(rows) for rows in halves]
        for c0 in blocks:
            stream.land(3)
            for n, rows in zip(norms, halves):
                up_block(n, rows, c0)
        for rows in halves:
            down(rows)

    @pl.when(jnp.logical_not(first))
    def _():
        up(halves[0])
        for prev, cur in zip(halves[:-1], halves[1:]):
            up(cur)
            down(prev)
        down(halves[-1])


def _ffn(h, mod, gain, wg, wu, wd, *, sub, next_gain=None, final_gain=None):
    bsz, s, d = h.shape
    d_ff = wg.shape[1]
    tm = FFN_TOKEN_TILE
    tile = pl.BlockSpec((1, tm, d), lambda b, i: (b, i, 0))
    hbm = pl.BlockSpec(memory_space=pl.ANY)
    in_specs = [tile, pl.BlockSpec((1, N_MOD, d), lambda b, i: (b, 0, 0)), _const_spec((1, d)),
                hbm, hbm, hbm]
    args = [h, mod, gain.reshape(1, d), wg, wu, wd]
    for extra in (next_gain, final_gain):
        if extra is not None:
            in_specs.append(_const_spec((1, d)))
            args.append(extra.reshape(1, d))
    out_specs, out_shape = [tile], [jax.ShapeDtypeStruct((bsz, s, d), F32)]
    if next_gain is not None:
        out_specs.append(pl.BlockSpec((1, d, tm), lambda b, i: (b, 0, i)))
        out_shape.append(jax.ShapeDtypeStruct((bsz, d, s), BF16))
    outs = pl.pallas_call(
        functools.partial(_ffn_kernel, sub=sub, emit_next=next_gain is not None,
                          final=final_gain is not None),
        grid=(bsz, s // tm),
        in_specs=in_specs,
        out_specs=out_specs,
        out_shape=out_shape,
        scratch_shapes=[
            pltpu.VMEM((tm, d_ff), BF16),
            pltpu.VMEM((d, 2 * d_ff), BF16), pltpu.VMEM((d_ff, d), BF16),
            pltpu.VMEM((LOAD_SLOTS, d, FF_CHUNK), F32), pltpu.VMEM((LOAD_SLOTS, FF_CHUNK, d), F32),
            pltpu.SemaphoreType.DMA((LOAD_SLOTS,)),
        ],
        compiler_params=pltpu.CompilerParams(
            dimension_semantics=("arbitrary", "arbitrary"),
            vmem_limit_bytes=V7X_VMEM_LIMIT),
        name="ffn%d" % sub,
    )(*args)
    return outs if next_gain is not None else outs[0]


def _rope_rows(x1, x2, cos, sin):
    return x1 * cos - x2 * sin, x2 * cos + x1 * sin


def _prep_kernel(*refs):
    for t0 in range(0, refs[0].shape[2], PREP_SUB_TILE):
        _prep_group(slice(t0, t0 + PREP_SUB_TILE), *refs)


def _prep_group(tok, n_ref, pos_ref, win_ref, wuq_ref, wukv_ref,
                tab_ref,
                qa_ref, ka_ref, va_ref, qb_ref, kb_ref, vb_ref, g_ref):
    tm = PREP_SUB_TILE
    n_t = n_ref[0, :, tok]

    def row_consts(name):
        lo, rows = PREP_TABLE[name]
        return _lane_tile(tab_ref[lo:lo + rows, :], tm)

    def proj_t(r0, r1):
        return jnp.dot(win_ref[r0:r1, :], n_t, preferred_element_type=F32)

    o_q, o_kv, o_kr = 0, MLA_Q_LORA, MLA_Q_LORA + MLA_KV_LORA
    o_qb = o_kr + MLA_ROPE
    w_b = DIFF_HEADS * DIFF_V
    o_kb, o_vb, o_g = o_qb + w_b, o_qb + 2 * w_b, o_qb + 3 * w_b
    g_chunk = g_ref.shape[1] // 4

    def emit_gates(i):
        r0 = i * g_chunk
        z = proj_t(o_g + r0, o_g + r0 + g_chunk)
        g_ref[0, r0:r0 + g_chunk, tok] = _sigmoid(z).astype(BF16)

    z_a = proj_t(o_q, o_qb)
    emit_gates(0)
    zq, zkv, kr = z_a[o_q:o_kv], z_a[o_kv:o_kr], z_a[o_kr:o_qb]
    rq = lax.rsqrt(jnp.mean(zq * zq, axis=0, keepdims=True) + NORM_EPS)
    zqn = (zq * rq * row_consts("qnorm")).astype(BF16)
    q_all = jnp.dot(wuq_ref[...], zqn, preferred_element_type=F32)
    rkv = lax.rsqrt(jnp.mean(zkv * zkv, axis=0, keepdims=True) + NORM_EPS)
    zkvn = (zkv * rkv * row_consts("kvnorm")).astype(BF16)
    kv_all = jnp.dot(wukv_ref[...], zkvn, preferred_element_type=F32)
    emit_gates(1)

    pos = pos_ref[0, :, tok].astype(F32)
    ang_a = pos * row_consts("freq_a")
    cos_a, sin_a = jnp.cos(ang_a), jnp.sin(ang_a)
    ang_b = pos * row_consts("freq_b")
    cos_b, sin_b = jnp.cos(ang_b), jnp.sin(ang_b)

    qgain_a = row_consts("qgain_a")
    kgain_a = row_consts("kgain_a")
    q_scale = LOG2E / math.sqrt(MLA_QK)
    pad_rows = jnp.zeros((HEAD_PAD - MLA_QK, tm), F32)
    sum_rows = (lax.broadcasted_iota(jnp.int32, (SUM_ROWS, tm), 0) == 0).astype(BF16)
    half = MLA_ROPE // 2

    def head_norm_rope_a(xh, gain):
        r = lax.rsqrt(jnp.sum(xh * xh, axis=0, keepdims=True) * (1.0 / MLA_QK) + NORM_EPS)
        xh = xh * r * gain
        r1, r2 = _rope_rows(xh[MLA_NOPE:MLA_NOPE + half], xh[MLA_NOPE + half:MLA_QK], cos_a, sin_a)
        return jnp.concatenate([xh[:MLA_NOPE], r1, r2, xh[MLA_QK:]], axis=0)

    zqb = proj_t(o_qb, o_kb)
    zkb = proj_t(o_kb, o_vb)
    for hd in range(MLA_HEADS):
        lo = hd * HEAD_PAD
        qh = head_norm_rope_a(q_all[lo:lo + HEAD_PAD], qgain_a) * q_scale
        qa_ref[0, lo:lo + HEAD_PAD, tok] = qh.astype(BF16)
        kvh = kv_all[lo:lo + HEAD_PAD]
        va_ref[0, hd * MLA_VS:hd * MLA_VS + MLA_V, tok] = kvh[MLA_NOPE:].astype(BF16)
        va_ref[0, hd * MLA_VS + MLA_V:(hd + 1) * MLA_VS, tok] = sum_rows
        kh = jnp.concatenate([kvh[:MLA_NOPE], kr, pad_rows], axis=0)
        kh = head_norm_rope_a(kh, kgain_a)
        ka_ref[0, tok, lo:lo + HEAD_PAD] = kh.T.astype(BF16)
    emit_gates(2)

    qgain_b = row_consts("qgain_b")
    kgain_b = row_consts("kgain_b")
    qb_scale = LOG2E / math.sqrt(DIFF_HD)
    hb = DIFF_ROT // 2

    def head_norm_rope_b(xh, gain):
        r = lax.rsqrt(jnp.mean(xh * xh, axis=0, keepdims=True) + NORM_EPS)
        xh = xh * r * gain
        r1, r2 = _rope_rows(xh[:hb], xh[hb:DIFF_ROT], cos_b, sin_b)
        return jnp.concatenate([r1, r2, xh[DIFF_ROT:]], axis=0)

    zvb = proj_t(o_vb, o_g)
    k_parts = []
    for blk in range(2 * DIFF_HEADS):
        lo = blk * DIFF_HD
        qh = head_norm_rope_b(zqb[lo:lo + DIFF_HD], qgain_b) * qb_scale
        qb_ref[0, lo:lo + DIFF_HD, tok] = qh.astype(BF16)
        k_parts.append(head_norm_rope_b(zkb[lo:lo + DIFF_HD], kgain_b))
    for hd in range(DIFF_HEADS):
        k12 = jnp.concatenate(k_parts[2 * hd:2 * hd + 2], axis=0)
        kb_ref[0, tok, hd * DIFF_V:(hd + 1) * DIFF_V] = k12.T.astype(BF16)
        vb_ref[0, hd * DIFF_VS:hd * DIFF_VS + DIFF_V, tok] = zvb[hd * DIFF_V:(hd + 1) * DIFF_V].astype(BF16)
        vb_ref[0, hd * DIFF_VS + DIFF_V:(hd + 1) * DIFF_VS, tok] = sum_rows
    emit_gates(3)


def _prep(n_t, pos3, w):
    bsz, d, s = n_t.shape
    tm = PREP_TOKEN_TILE
    n_in = w["win_t"].shape[0]
    wa, wb = MLA_HEADS * HEAD_PAD, DIFF_HEADS * DIFF_V
    n_gate = n_in - (MLA_Q_LORA + MLA_KV_LORA + MLA_ROPE + 3 * wb)

    def fm(rows):
        return (jax.ShapeDtypeStruct((bsz, rows, s), BF16),
                pl.BlockSpec((1, rows, tm), lambda b, i: (b, 0, i)))

    def tmaj(cols):
        return (jax.ShapeDtypeStruct((bsz, s, cols), BF16),
                pl.BlockSpec((1, tm, cols), lambda b, i: (b, i, 0)))

    outs = [fm(wa), tmaj(wa), fm(MLA_HEADS * MLA_VS), fm(wb), tmaj(wb), fm(DIFF_HEADS * DIFF_VS), fm(n_gate)]
    consts = [w["win_t"], w["wuq_t"], w["wukv_t"], w["table"]]
    in_specs = [
        pl.BlockSpec((1, d, tm), lambda b, i: (b, 0, i)),
        pl.BlockSpec((1, 1, tm), lambda b, i: (b, 0, i)),
    ] + [_const_spec(a.shape) for a in consts]
    return pl.pallas_call(
        _prep_kernel,
        grid=(bsz, s // tm),
        in_specs=in_specs,
        out_specs=[o[1] for o in outs],
        out_shape=[o[0] for o in outs],
        compiler_params=pltpu.CompilerParams(
            dimension_semantics=("arbitrary", "arbitrary"),
            vmem_limit_bytes=V7X_VMEM_LIMIT),
        name="prep",
    )(n_t, pos3, *consts)


def _normalise(acc, rows):
    return acc[:rows] * (1.0 / acc[rows:rows + 1])


def _attention_pipeline(n_heads, s_len, tq, kc, n_streams, scores_fn, v_fn, finish_fn):
    units = [(hd, q0, c0) for hd in range(n_heads) for q0 in range(0, s_len, tq)
             for c0 in range(0, s_len, kc)]
    streams = range(n_streams)
    n = len(units)
    s_cur = [scores_fn(*units[0], t) for t in streams]
    m = [None] * n_streams
    acc = [None] * n_streams
    pending = None
    for i in range(n + 1):
        s_nxt = [scores_fn(*units[i + 1], t) for t in streams] if i + 1 < n else None
        if pending is not None:
            (hd, q0, c0), alphas, probs = pending
            v = v_fn(hd, c0)
            for t in streams:
                o = jnp.dot(v, probs[t], preferred_element_type=F32)
                acc[t] = o if alphas[t] is None else alphas[t] * acc[t] + o
            if c0 + kc == s_len:
                finish_fn(hd, q0, acc)
                acc = [None] * n_streams
            pending = None
        if i < n:
            _, _, c0 = units[i]
            alphas, probs = [], []
            for t in streams:
                cmax = jnp.max(s_cur[t], axis=0, keepdims=True)
                if c0 == 0:
                    m[t] = cmax
                    alphas.append(None)
                else:
                    m_new = jnp.maximum(m[t], cmax)
                    alphas.append(jnp.exp2(m[t] - m_new))
                    m[t] = m_new
                probs.append(jnp.exp2(s_cur[t] - m[t]).astype(BF16))
            pending = (units[i], alphas, probs)
            s_cur = s_nxt


def _mla_kernel(k_ref, q_ref, v_ref, o_ref):
    def scores(hd, q0, c0, t):
        return jnp.dot(k_ref[0, c0:c0 + MLA_KEY_CHUNK, hd * HEAD_PAD:(hd + 1) * HEAD_PAD],
                       q_ref[0, hd * HEAD_PAD:(hd + 1) * HEAD_PAD, q0:q0 + MLA_Q_TILE],
                       preferred_element_type=F32)

    def values(hd, c0):
        return v_ref[0, hd * MLA_VS:(hd + 1) * MLA_VS, c0:c0 + MLA_KEY_CHUNK]

    def finish(hd, q0, accs):
        o_ref[0, hd * MLA_V:(hd + 1) * MLA_V, q0:q0 + MLA_Q_TILE] = (
            _normalise(accs[0], MLA_V).astype(BF16))

    _attention_pipeline(MLA_HEADS_PER_STEP, k_ref.shape[1], MLA_Q_TILE, MLA_KEY_CHUNK, 1,
                        scores, values, finish)


def _mla_attention(ka, qa, va):
    bsz, s, _ = ka.shape
    hp = MLA_HEADS_PER_STEP
    return pl.pallas_call(
        _mla_kernel,
        grid=(bsz, MLA_HEADS // hp),
        in_specs=[
            pl.BlockSpec((1, s, hp * HEAD_PAD), lambda b, h: (b, 0, h)),
            pl.BlockSpec((1, hp * HEAD_PAD, s), lambda b, h: (b, h, 0)),
            pl.BlockSpec((1, hp * MLA_VS, s), lambda b, h: (b, h, 0)),
        ],
        out_specs=pl.BlockSpec((1, hp * MLA_V, s), lambda b, h: (b, h, 0)),
        out_shape=jax.ShapeDtypeStruct((bsz, MLA_HEADS * MLA_V, s), BF16),
        compiler_params=pltpu.CompilerParams(
            dimension_semantics=("arbitrary", "arbitrary"),
            vmem_limit_bytes=V7X_VMEM_LIMIT),
        name="mla_attn",
    )(ka, qa, va)


def _diff_kernel(k_ref, q_ref, v_ref, lq1_ref, lk1_ref, lq2_ref, lk2_ref, subln_ref, o_ref, *,
                 lambda_init):
    lam = (jnp.exp(jnp.sum(lq1_ref[...] * lk1_ref[...], axis=-1, keepdims=True))
           - jnp.exp(jnp.sum(lq2_ref[...] * lk2_ref[...], axis=-1, keepdims=True))
           + lambda_init)
    subln = _lane_tile(subln_ref[...], DIFF_Q_TILE)
    zeros = jnp.zeros((DIFF_HD, DIFF_Q_TILE), BF16)

    def scores(hd, q0, c0, t):
        lo = hd * DIFF_V
        q12 = q_ref[0, lo:lo + DIFF_V, q0:q0 + DIFF_Q_TILE]
        k12 = k_ref[0, c0:c0 + DIFF_KEY_CHUNK, lo:lo + DIFF_V]
        if t == 0:
            q = jnp.concatenate([q12[:DIFF_HD], zeros], axis=0)
        else:
            q = jnp.concatenate([zeros, q12[DIFF_HD:]], axis=0)
        return jnp.dot(k12, q, preferred_element_type=F32)

    def values(hd, c0):
        return v_ref[0, hd * DIFF_VS:(hd + 1) * DIFF_VS, c0:c0 + DIFF_KEY_CHUNK]

    def finish(hd, q0, accs):
        o = _normalise(accs[0], DIFF_V) - lam * _normalise(accs[1], DIFF_V)
        r = lax.rsqrt(jnp.mean(o * o, axis=0, keepdims=True) + NORM_EPS)
        o_ref[0, hd * DIFF_V:(hd + 1) * DIFF_V, q0:q0 + DIFF_Q_TILE] = (
            (o * r * subln) * (1.0 - lambda_init)).astype(BF16)

    _attention_pipeline(DIFF_HEADS_PER_STEP, k_ref.shape[1], DIFF_Q_TILE, DIFF_KEY_CHUNK, 2,
                        scores, values, finish)


def _diff_attention(kb, qb, vb, lq1, lk1, lq2, lk2, subln, lambda_init):
    bsz, s, _ = kb.shape
    hp = DIFF_HEADS_PER_STEP
    vec = _const_spec((1, DIFF_HD))
    return pl.pallas_call(
        functools.partial(_diff_kernel, lambda_init=lambda_init),
        grid=(bsz, DIFF_HEADS // hp),
        in_specs=[
            pl.BlockSpec((1, s, hp * DIFF_V), lambda b, h: (b, 0, h)),
            pl.BlockSpec((1, hp * DIFF_V, s), lambda b, h: (b, h, 0)),
            pl.BlockSpec((1, hp * DIFF_VS, s), lambda b, h: (b, h, 0)),
            vec, vec, vec, vec,
            _const_spec((DIFF_V, LANES)),
        ],
        out_specs=pl.BlockSpec((1, hp * DIFF_V, s), lambda b, h: (b, h, 0)),
        out_shape=jax.ShapeDtypeStruct((bsz, DIFF_HEADS * DIFF_V, s), BF16),
        compiler_params=pltpu.CompilerParams(
            dimension_semantics=("arbitrary", "arbitrary"),
            vmem_limit_bytes=V7X_VMEM_LIMIT),
        name="diff_attn",
    )(kb, qb, vb, lq1, lk1, lq2, lk2, subln)


def _merge_kernel(oa_ref, ob_ref, g_ref, h_ref, mod_ref, woa_ref, wob_ref, wout_ref, o_ref):
    d = h_ref.shape[2]
    gate = mod_ref[0, 5:6, :]
    groups = [slice(t0, t0 + MERGE_SUB_TILE) for t0 in range(0, h_ref.shape[1], MERGE_SUB_TILE)]

    def branch_mix(tok):
        ya = jnp.dot(woa_ref[...], oa_ref[0, :, tok], preferred_element_type=F32)
        yb = jnp.dot(wob_ref[...], ob_ref[0, :, tok], preferred_element_type=F32)
        ga = g_ref[0, :d, tok].astype(F32)
        gb = g_ref[0, d:, tok].astype(F32)
        return (ga * ya + gb * yb).astype(BF16)

    def project(tok, mix):
        y_t = jnp.dot(wout_ref[...], mix, preferred_element_type=F32)
        o_ref[0, tok, :] = h_ref[0, tok, :] + gate * y_t.T

    mixes = [branch_mix(groups[0])]
    for prev, cur in zip(groups[:-1], groups[1:]):
        mixes.append(branch_mix(cur))
        project(prev, mixes[-2])
    project(groups[-1], mixes[-1])


def _merge(oa, ob, g, h, mod, woa_t, wob_t, wout_t):
    bsz, s, d = h.shape
    tm = MERGE_TOKEN_TILE
    return pl.pallas_call(
        _merge_kernel,
        grid=(bsz, s // tm),
        in_specs=[
            pl.BlockSpec((1, oa.shape[1], tm), lambda b, i: (b, 0, i)),
            pl.BlockSpec((1, ob.shape[1], tm), lambda b, i: (b, 0, i)),
            pl.BlockSpec((1, g.shape[1], tm), lambda b, i: (b, 0, i)),
            pl.BlockSpec((1, tm, d), lambda b, i: (b, i, 0)),
            pl.BlockSpec((1, N_MOD, d), lambda b, i: (b, 0, 0)),
            _const_spec(woa_t.shape),
            _const_spec(wob_t.shape),
            _const_spec(wout_t.shape),
        ],
        out_specs=pl.BlockSpec((1, tm, d), lambda b, i: (b, i, 0)),
        out_shape=jax.ShapeDtypeStruct((bsz, s, d), F32),
        compiler_params=pltpu.CompilerParams(
            dimension_semantics=("arbitrary", "arbitrary"),
            vmem_limit_bytes=V7X_VMEM_LIMIT),
        name="merge",
    )(oa, ob, g, h, mod, woa_t, wob_t, wout_t)


def _lane_bcast(v, rows=None):
    n = v.shape[0]
    out = jnp.broadcast_to(v.astype(F32)[:, None], (n, LANES))
    if rows is not None and rows > n:
        out = jnp.pad(out, ((0, rows - n), (0, 0)))
    return out


def _rope_freqs(half, theta):
    return 1.0 / (theta ** (jnp.arange(half, dtype=F32) / half))


def _prep_table(vectors):
    parts = []
    for name, (_, rows) in PREP_TABLE.items():
        v = vectors[name].astype(F32)
        parts.append(jnp.pad(v, (0, rows - v.shape[0])))
    return jnp.concatenate(parts)


def _layer_weights(l, w_in, mla_q_norm, mla_w_uq, mla_kv_norm, mla_w_ukv, mla_q_gain, mla_k_gain,
                   diff_q_gain, diff_k_gain):
    wuq = mla_w_uq[l].reshape(MLA_Q_LORA, MLA_HEADS, MLA_QK)
    wuq = jnp.pad(wuq, ((0, 0), (0, 0), (0, HEAD_PAD - MLA_QK))).reshape(MLA_Q_LORA, -1)
    return {
        "win_t": w_in[l].T.astype(BF16),
        "wuq_t": wuq.T.astype(BF16),
        "wukv_t": mla_w_ukv[l].T.astype(BF16),
        "table": _lane_bcast(_prep_table({
            "qnorm": mla_q_norm[l], "kvnorm": mla_kv_norm[l],
            "qgain_a": mla_q_gain[l], "kgain_a": mla_k_gain[l],
            "qgain_b": diff_q_gain[l], "kgain_b": diff_k_gain[l],
            "freq_a": _rope_freqs(MLA_ROPE // 2, MLA_THETA),
            "freq_b": _rope_freqs(DIFF_ROT // 2, DIFF_THETA)})),
    }


def kernel(x, c, positions, w_ada, b_ada, ffn1_norm, ffn1_w_gate, ffn1_w_up, ffn1_w_down, mix_norm, w_in, mla_q_norm, mla_w_uq, mla_kv_norm, mla_w_ukv, mla_q_gain, mla_k_gain, mla_w_o, diff_q_gain, diff_k_gain, diff_lambda_q1, diff_lambda_k1, diff_lambda_q2, diff_lambda_k2, diff_subln, diff_w_o, w_out, ffn2_norm, ffn2_w_gate, ffn2_w_up, ffn2_w_down, final_norm):
    bsz, s, d = x.shape
    depth = w_ada.shape[0]
    pos3 = positions.reshape(bsz, 1, s)
    h = x
    for l in range(depth):
        lambda_init = 0.8 - 0.6 * math.exp(-0.3 * l)
        mod = _ada(c, w_ada[l], b_ada[l]).reshape(bsz, N_MOD, d)

        h, n_mix = _ffn(h, mod, ffn1_norm[l], ffn1_w_gate[l], ffn1_w_up[l], ffn1_w_down[l],
                        sub=0, next_gain=mix_norm[l])

        w = _layer_weights(l, w_in, mla_q_norm, mla_w_uq, mla_kv_norm, mla_w_ukv, mla_q_gain,
                           mla_k_gain, diff_q_gain, diff_k_gain)
        qa, ka, va, qb, kb, vb, g = _prep(n_mix, pos3, w)
        oa = _mla_attention(ka, qa, va)
        ob = _diff_attention(kb, qb, vb,
                             diff_lambda_q1[l].reshape(1, -1), diff_lambda_k1[l].reshape(1, -1),
                             diff_lambda_q2[l].reshape(1, -1), diff_lambda_k2[l].reshape(1, -1),
                             _lane_bcast(diff_subln[l]), lambda_init)
        h = _merge(oa, ob, g, h, mod, mla_w_o[l].T.astype(BF16), diff_w_o[l].T.astype(BF16),
                   w_out[l].T.astype(BF16))

        h = _ffn(h, mod, ffn2_norm[l], ffn2_w_gate[l], ffn2_w_up[l], ffn2_w_down[l],
                 sub=2, final_gain=final_norm[l])
    return h
```

```python
import functools
import math

import jax
import jax.numpy as jnp
from jax import lax
from jax.experimental import pallas as pl
from jax.experimental.pallas import tpu as pltpu

F32 = jnp.float32
BF16 = jnp.bfloat16

NORM_EPS = 1e-6
N_MOD = 9

MLA_HEADS = 8
MLA_NOPE = 64
MLA_ROPE = 32
MLA_QK = MLA_NOPE + MLA_ROPE
MLA_V = 64
MLA_Q_LORA = 384
MLA_KV_LORA = 256
MLA_THETA = 10000.0
DIFF_HEADS = 4
DIFF_HD = 64
DIFF_V = 2 * DIFF_HD
DIFF_THETA = 500000.0
DIFF_ROT = DIFF_HD // 4

LANES = 128
HEAD_PAD = 128
SUM_ROWS = 16
MLA_VS = MLA_V + SUM_ROWS
DIFF_VS = DIFF_V + SUM_ROWS
LOG2E = math.log2(math.e)
V7X_VMEM_LIMIT = 56 * 1024 * 1024

def _table_layout(sizes):
    layout, lo = {}, 0
    for name, rows in sizes:
        layout[name] = (lo, rows)
        lo += rows
    return layout


PREP_TABLE = _table_layout([
    ("qnorm", MLA_Q_LORA), ("kvnorm", MLA_KV_LORA), ("qgain_a", HEAD_PAD), ("kgain_a", HEAD_PAD),
    ("qgain_b", DIFF_HD), ("kgain_b", DIFF_HD), ("freq_a", MLA_ROPE // 2), ("freq_b", DIFF_ROT // 2)])

PREP_TOKEN_TILE = 1024
PREP_SUB_TILE = 1024
FFN_TOKEN_TILE = 1024
FFN_SUB_TILE = 512
MERGE_TOKEN_TILE = 1024
MERGE_SUB_TILE = 512
MLA_Q_TILE = 512
DIFF_Q_TILE = 512
MLA_KEY_CHUNK = 256
DIFF_KEY_CHUNK = 512
MLA_HEADS_PER_STEP = 4
DIFF_HEADS_PER_STEP = 2
FF_CHUNK = 256
LOAD_SLOTS = 5
ADA_COL_BLOCK = 2304


def _sigmoid(x):
    return 1.0 / (1.0 + jnp.exp(-x))


def _rms_rows(x, gain):
    ms = jnp.mean(x * x, axis=-1, keepdims=True)
    return x * lax.rsqrt(ms + NORM_EPS) * gain


def _lane_tile(g, width):
    return jnp.tile(g, (1, width // LANES))


def _const_spec(shape):
    return pl.BlockSpec(shape, lambda *_: (0,) * len(shape), pipeline_mode=pl.Buffered(1))


def _ada_kernel(c_ref, w_ref, b_ref, o_ref):
    c = c_ref[...]
    bsz = c.shape[0]
    cond = c * _sigmoid(c)
    c_hi = cond.astype(BF16).astype(F32)
    lhs = jnp.concatenate([c_hi, cond - c_hi], axis=0).astype(BF16)
    w = w_ref[...]
    w_hi = w.astype(BF16)
    w_lo = (w - w_hi.astype(F32)).astype(BF16)
    a = jnp.dot(lhs, w_hi, preferred_element_type=F32)
    b = jnp.dot(lhs, w_lo, preferred_element_type=F32)
    o_ref[...] = a[:bsz] + a[bsz:] + b[:bsz] + b_ref[...]


def _ada(c, w_ada, b_ada):
    bsz, d = c.shape
    cols = w_ada.shape[1]
    return pl.pallas_call(
        _ada_kernel,
        grid=(cols // ADA_COL_BLOCK,),
        in_specs=[
            pl.BlockSpec((bsz, d), lambda j: (0, 0)),
            pl.BlockSpec((d, ADA_COL_BLOCK), lambda j: (0, j)),
            pl.BlockSpec((1, ADA_COL_BLOCK), lambda j: (0, j)),
        ],
        out_specs=pl.BlockSpec((bsz, ADA_COL_BLOCK), lambda j: (0, j)),
        out_shape=jax.ShapeDtypeStruct((bsz, cols), F32),
        compiler_params=pltpu.CompilerParams(dimension_semantics=("arbitrary",),
                                             vmem_limit_bytes=V7X_VMEM_LIMIT),
        name="ada",
    )(c, w_ada, b_ada.reshape(1, cols))


def _adaln(x, gain, mod_ref, sub, dtype=BF16):
    shift = mod_ref[0, 3 * sub:3 * sub + 1, :]
    scale = mod_ref[0, 3 * sub + 1:3 * sub + 2, :]
    return (_rms_rows(x, gain) * (1.0 + scale) + shift).astype(dtype)


class _WeightStream:
    def __init__(self, wg_hbm, wu_hbm, wd_hbm, wgu_s, wd_s, stage_c, stage_r, sem):
        self.sem = sem
        self.jobs = []
        for c0 in range(0, wd_s.shape[0], FF_CHUNK):
            blk = slice(c0, c0 + FF_CHUNK)
            self.jobs.append((wg_hbm.at[:, blk], stage_c, wgu_s.at[:, 2 * c0:2 * c0 + FF_CHUNK]))
            self.jobs.append((wu_hbm.at[:, blk], stage_c,
                              wgu_s.at[:, 2 * c0 + FF_CHUNK:2 * (c0 + FF_CHUNK)]))
            self.jobs.append((wd_hbm.at[blk, :], stage_r, wd_s.at[blk, :]))
        self.landed = 0
        for i in range(min(LOAD_SLOTS - 1, len(self.jobs))):
            self._copy(i).start()

    def _copy(self, i):
        src, stage, _ = self.jobs[i]
        return pltpu.make_async_copy(src, stage.at[i % LOAD_SLOTS], self.sem.at[i % LOAD_SLOTS])

    def land(self, count):
        for i in range(self.landed, self.landed + count):
            if i + LOAD_SLOTS - 1 < len(self.jobs):
                self._copy(i + LOAD_SLOTS - 1).start()
            self._copy(i).wait()
            _, stage, dst = self.jobs[i]
            dst[...] = stage[i % LOAD_SLOTS].astype(BF16)
        self.landed += count


def _ffn_kernel(*refs, sub, emit_next, final):
    refs = list(refs)
    x_ref, mod_ref, gain_ref = refs.pop(0), refs.pop(0), refs.pop(0)
    w_hbm = [refs.pop(0) for _ in range(3)]
    next_gain_ref = refs.pop(0) if emit_next else None
    fgain_ref = refs.pop(0) if final else None
    o_ref = refs.pop(0)
    n_next_ref = refs.pop(0) if emit_next else None
    a_ref, wgu_ref, wd_ref, stage_c, stage_r, sem = refs

    d_ff = wd_ref.shape[0]
    gate = mod_ref[0, 3 * sub + 2:3 * sub + 3, :]
    halves = [slice(r0, r0 + FFN_SUB_TILE) for r0 in range(0, x_ref.shape[1], FFN_SUB_TILE)]
    blocks = range(0, d_ff, FF_CHUNK)
    first = (pl.program_id(0) == 0) & (pl.program_id(1) == 0)

    def norm(rows):
        return _adaln(x_ref[0, rows, :], gain_ref[...], mod_ref, sub)

    def up_block(n, rows, c0):
        gu = jnp.dot(n, wgu_ref[:, 2 * c0:2 * (c0 + FF_CHUNK)], preferred_element_type=F32)
        g, u = gu[:, :FF_CHUNK], gu[:, FF_CHUNK:]
        a_ref[rows, c0:c0 + FF_CHUNK] = ((g * _sigmoid(g)) * u).astype(BF16)

    def up(rows):
        n = norm(rows)
        for c0 in blocks:
            up_block(n, rows, c0)

    def down(rows):
        f = jnp.dot(a_ref[rows, :], wd_ref[...], preferred_element_type=F32)
        h = x_ref[0, rows, :] + (0.5 * gate) * f
        if final:
            h = _rms_rows(h, fgain_ref[...])
        o_ref[0, rows, :] = h
        if emit_next:
            n_next_ref[0, :, rows] = _adaln(h, next_gain_ref[...], mod_ref, sub + 1, F32).T.astype(BF16)

    @pl.when(first)
    def _():
        stream = _WeightStream(*w_hbm, wgu_ref, wd_ref, stage_c, stage_r, sem)
        norms = [norm(rows) for rows in halves]
        for c0 in blocks:
            stream.land(3)
            for n, rows in zip(norms, halves):
                up_block(n, rows, c0)
        for rows in halves:
            down(rows)

    @pl.when(jnp.logical_not(first))
    def _():
        up(halves[0])
        for prev, cur in zip(halves[:-1], halves[1:]):
            up(cur)
            down(prev)
        down(halves[-1])


def _ffn(h, mod, gain, wg, wu, wd, *, sub, next_gain=None, final_gain=None):
    bsz, s, d = h.shape
    d_ff = wg.shape[1]
    tm = FFN_TOKEN_TILE
    tile = pl.BlockSpec((1, tm, d), lambda b, i: (b, i, 0))
    hbm = pl.BlockSpec(memory_space=pl.ANY)
    in_specs = [tile, pl.BlockSpec((1, N_MOD, d), lambda b, i: (b, 0, 0)), _const_spec((1, d)),
                hbm, hbm, hbm]
    args = [h, mod, gain.reshape(1, d), wg, wu, wd]
    for extra in (next_gain, final_gain):
        if extra is not None:
            in_specs.append(_const_spec((1, d)))
            args.append(extra.reshape(1, d))
    out_specs, out_shape = [tile], [jax.ShapeDtypeStruct((bsz, s, d), F32)]
    if next_gain is not None:
        out_specs.append(pl.BlockSpec((1, d, tm), lambda b, i: (b, 0, i)))
        out_shape.append(jax.ShapeDtypeStruct((bsz, d, s), BF16))
    outs = pl.pallas_call(
        functools.partial(_ffn_kernel, sub=sub, emit_next=next_gain is not None,
                          final=final_gain is not None),
        grid=(bsz, s // tm),
        in_specs=in_specs,
        out_specs=out_specs,
        out_shape=out_shape,
        scratch_shapes=[
            pltpu.VMEM((tm, d_ff), BF16),
            pltpu.VMEM((d, 2 * d_ff), BF16), pltpu.VMEM((d_ff, d), BF16),
            pltpu.VMEM((LOAD_SLOTS, d, FF_CHUNK), F32), pltpu.VMEM((LOAD_SLOTS, FF_CHUNK, d), F32),
            pltpu.SemaphoreType.DMA((LOAD_SLOTS,)),
        ],
        compiler_params=pltpu.CompilerParams(
            dimension_semantics=("arbitrary", "arbitrary"),
            vmem_limit_bytes=V7X_VMEM_LIMIT),
        name="ffn%d" % sub,
    )(*args)
    return outs if next_gain is not None else outs[0]


def _rope_rows(x1, x2, cos, sin):
    return x1 * cos - x2 * sin, x2 * cos + x1 * sin


def _prep_kernel(*refs):
    for t0 in range(0, refs[0].shape[2], PREP_SUB_TILE):
        _prep_group(slice(t0, t0 + PREP_SUB_TILE), *refs)


def _prep_group(tok, n_ref, pos_ref, win_ref, wuq_ref, wukv_ref,
                tab_ref,
                qa_ref, ka_ref, va_ref, qb_ref, kb_ref, vb_ref, g_ref):
    tm = PREP_SUB_TILE
    n_t = n_ref[0, :, tok]

    def row_consts(name):
        lo, rows = PREP_TABLE[name]
        return _lane_tile(tab_ref[lo:lo + rows, :], tm)

    def proj_t(r0, r1):
        return jnp.dot(win_ref[r0:r1, :], n_t, preferred_element_type=F32)

    o_q, o_kv, o_kr = 0, MLA_Q_LORA, MLA_Q_LORA + MLA_KV_LORA
    o_qb = o_kr + MLA_ROPE
    w_b = DIFF_HEADS * DIFF_V
    o_kb, o_vb, o_g = o_qb + w_b, o_qb + 2 * w_b, o_qb + 3 * w_b
    g_chunk = g_ref.shape[1] // 4

    def emit_gates(i):
        r0 = i * g_chunk
        z = proj_t(o_g + r0, o_g + r0 + g_chunk)
        g_ref[0, r0:r0 + g_chunk, tok] = _sigmoid(z).astype(BF16)

    z_a = proj_t(o_q, o_qb)
    emit_gates(0)
    zq, zkv, kr = z_a[o_q:o_kv], z_a[o_kv:o_kr], z_a[o_kr:o_qb]
    rq = lax.rsqrt(jnp.mean(zq * zq, axis=0, keepdims=True) + NORM_EPS)
    zqn = (zq * rq * row_consts("qnorm")).astype(BF16)
    q_all = jnp.dot(wuq_ref[...], zqn, preferred_element_type=F32)
    rkv = lax.rsqrt(jnp.mean(zkv * zkv, axis=0, keepdims=True) + NORM_EPS)
    zkvn = (zkv * rkv * row_consts("kvnorm")).astype(BF16)
    kv_all = jnp.dot(wukv_ref[...], zkvn, preferred_element_type=F32)
    emit_gates(1)

    pos = pos_ref[0, :, tok].astype(F32)
    ang_a = pos * row_consts("freq_a")
    cos_a, sin_a = jnp.cos(ang_a), jnp.sin(ang_a)
    ang_b = pos * row_consts("freq_b")
    cos_b, sin_b = jnp.cos(ang_b), jnp.sin(ang_b)

    qgain_a = row_consts("qgain_a")
    kgain_a = row_consts("kgain_a")
    q_scale = LOG2E / math.sqrt(MLA_QK)
    pad_rows = jnp.zeros((HEAD_PAD - MLA_QK, tm), F32)
    sum_rows = (lax.broadcasted_iota(jnp.int32, (SUM_ROWS, tm), 0) == 0).astype(BF16)
    half = MLA_ROPE // 2

    def head_norm_rope_a(xh, gain):
        r = lax.rsqrt(jnp.sum(xh * xh, axis=0, keepdims=True) * (1.0 / MLA_QK) + NORM_EPS)
        xh = xh * r * gain
        r1, r2 = _rope_rows(xh[MLA_NOPE:MLA_NOPE + half], xh[MLA_NOPE + half:MLA_QK], cos_a, sin_a)
        return jnp.concatenate([xh[:MLA_NOPE], r1, r2, xh[MLA_QK:]], axis=0)

    zqb = proj_t(o_qb, o_kb)
    zkb = proj_t(o_kb, o_vb)
    for hd in range(MLA_HEADS):
        lo = hd * HEAD_PAD
        qh = head_norm_rope_a(q_all[lo:lo + HEAD_PAD], qgain_a) * q_scale
        qa_ref[0, lo:lo + HEAD_PAD, tok] = qh.astype(BF16)
        kvh = kv_all[lo:lo + HEAD_PAD]
        va_ref[0, hd * MLA_VS:hd * MLA_VS + MLA_V, tok] = kvh[MLA_NOPE:].astype(BF16)
        va_ref[0, hd * MLA_VS + MLA_V:(hd + 1) * MLA_VS, tok] = sum_rows
        kh = jnp.concatenate([kvh[:MLA_NOPE], kr, pad_rows], axis=0)
        kh = head_norm_rope_a(kh, kgain_a)
        ka_ref[0, tok, lo:lo + HEAD_PAD] = kh.T.astype(BF16)
    emit_gates(2)

    qgain_b = row_consts("qgain_b")
    kgain_b = row_consts("kgain_b")
    qb_scale = LOG2E / math.sqrt(DIFF_HD)
    hb = DIFF_ROT // 2

    def head_norm_rope_b(xh, gain):
        r = lax.rsqrt(jnp.mean(xh * xh, axis=0, keepdims=True) + NORM_EPS)
        xh = xh * r * gain
        r1, r2 = _rope_rows(xh[:hb], xh[hb:DIFF_ROT], cos_b, sin_b)
        return jnp.concatenate([r1, r2, xh[DIFF_ROT:]], axis=0)

    zvb = proj_t(o_vb, o_g)
    k_parts = []
    for blk in range(2 * DIFF_HEADS):
        lo = blk * DIFF_HD
        qh = head_norm_rope_b(zqb[lo:lo + DIFF_HD], qgain_b) * qb_scale
        qb_ref[0, lo:lo + DIFF_HD, tok] = qh.astype(BF16)
        k_parts.append(head_norm_rope_b(zkb[lo:lo + DIFF_HD], kgain_b))
    for hd in range(DIFF_HEADS):
        k12 = jnp.concatenate(k_parts[2 * hd:2 * hd + 2], axis=0)
        kb_ref[0, tok, hd * DIFF_V:(hd + 1) * DIFF_V] = k12.T.astype(BF16)
        vb_ref[0, hd * DIFF_VS:hd * DIFF_VS + DIFF_V, tok] = zvb[hd * DIFF_V:(hd + 1) * DIFF_V].astype(BF16)
        vb_ref[0, hd * DIFF_VS + DIFF_V:(hd + 1) * DIFF_VS, tok] = sum_rows
    emit_gates(3)


def _prep(n_t, pos3, w):
    bsz, d, s = n_t.shape
    tm = PREP_TOKEN_TILE
    n_in = w["win_t"].shape[0]
    wa, wb = MLA_HEADS * HEAD_PAD, DIFF_HEADS * DIFF_V
    n_gate = n_in - (MLA_Q_LORA + MLA_KV_LORA + MLA_ROPE + 3 * wb)

    def fm(rows):
        return (jax.ShapeDtypeStruct((bsz, rows, s), BF16),
                pl.BlockSpec((1, rows, tm), lambda b, i: (b, 0, i)))

    def tmaj(cols):
        return (jax.ShapeDtypeStruct((bsz, s, cols), BF16),
                pl.BlockSpec((1, tm, cols), lambda b, i: (b, i, 0)))

    outs = [fm(wa), tmaj(wa), fm(MLA_HEADS * MLA_VS), fm(wb), tmaj(wb), fm(DIFF_HEADS * DIFF_VS), fm(n_gate)]
    consts = [w["win_t"], w["wuq_t"], w["wukv_t"], w["table"]]
    in_specs = [
        pl.BlockSpec((1, d, tm), lambda b, i: (b, 0, i)),
        pl.BlockSpec((1, 1, tm), lambda b, i: (b, 0, i)),
    ] + [_const_spec(a.shape) for a in consts]
    return pl.pallas_call(
        _prep_kernel,
        grid=(bsz, s // tm),
        in_specs=in_specs,
        out_specs=[o[1] for o in outs],
        out_shape=[o[0] for o in outs],
        compiler_params=pltpu.CompilerParams(
            dimension_semantics=("arbitrary", "arbitrary"),
            vmem_limit_bytes=V7X_VMEM_LIMIT),
        name="prep",
    )(n_t, pos3, *consts)


def _normalise(acc, rows):
    return acc[:rows] * (1.0 / acc[rows:rows + 1])


def _attention_pipeline(n_heads, s_len, tq, kc, n_streams, scores_fn, v_fn, finish_fn):
    units = [(hd, q0, c0) for hd in range(n_heads) for q0 in range(0, s_len, tq)
             for c0 in range(0, s_len, kc)]
    streams = range(n_streams)
    n = len(units)
    s_cur = [scores_fn(*units[0], t) for t in streams]
    m = [None] * n_streams
    acc = [None] * n_streams
    pending = None
    for i in range(n + 1):
        s_nxt = [scores_fn(*units[i + 1], t) for t in streams] if i + 1 < n else None
        if pending is not None:
            (hd, q0, c0), alphas, probs = pending
            v = v_fn(hd, c0)
            for t in streams:
                o = jnp.dot(v, probs[t], preferred_element_type=F32)
                acc[t] = o if alphas[t] is None else alphas[t] * acc[t] + o
            if c0 + kc == s_len:
                finish_fn(hd, q0, acc)
                acc = [None] * n_streams
            pending = None
        if i < n:
            _, _, c0 = units[i]
            alphas, probs = [], []
            for t in streams:
                cmax = jnp.max(s_cur[t], axis=0, keepdims=True)
                if c0 == 0:
                    m[t] = cmax
                    alphas.append(None)
                else:
                    m_new = jnp.maximum(m[t], cmax)
                    alphas.append(jnp.exp2(m[t] - m_new))
                    m[t] = m_new
                probs.append(jnp.exp2(s_cur[t] - m[t]).astype(BF16))
            pending = (units[i], alphas, probs)
            s_cur = s_nxt


def _mla_kernel(k_ref, q_ref, v_ref, o_ref):
    def scores(hd, q0, c0, t):
        return jnp.dot(k_ref[0, c0:c0 + MLA_KEY_CHUNK, hd * HEAD_PAD:(hd + 1) * HEAD_PAD],
                       q_ref[0, hd * HEAD_PAD:(hd + 1) * HEAD_PAD, q0:q0 + MLA_Q_TILE],
                       preferred_element_type=F32)

    def values(hd, c0):
        return v_ref[0, hd * MLA_VS:(hd + 1) * MLA_VS, c0:c0 + MLA_KEY_CHUNK]

    def finish(hd, q0, accs):
        o_ref[0, hd * MLA_V:(hd + 1) * MLA_V, q0:q0 + MLA_Q_TILE] = (
            _normalise(accs[0], MLA_V).astype(BF16))

    _attention_pipeline(MLA_HEADS_PER_STEP, k_ref.shape[1], MLA_Q_TILE, MLA_KEY_CHUNK, 1,
                        scores, values, finish)


def _mla_attention(ka, qa, va):
    bsz, s, _ = ka.shape
    hp = MLA_HEADS_PER_STEP
    return pl.pallas_call(
        _mla_kernel,
        grid=(bsz, MLA_HEADS // hp),
        in_specs=[
            pl.BlockSpec((1, s, hp * HEAD_PAD), lambda b, h: (b, 0, h)),
            pl.BlockSpec((1, hp * HEAD_PAD, s), lambda b, h: (b, h, 0)),
            pl.BlockSpec((1, hp * MLA_VS, s), lambda b, h: (b, h, 0)),
        ],
        out_specs=pl.BlockSpec((1, hp * MLA_V, s), lambda b, h: (b, h, 0)),
        out_shape=jax.ShapeDtypeStruct((bsz, MLA_HEADS * MLA_V, s), BF16),
        compiler_params=pltpu.CompilerParams(
            dimension_semantics=("arbitrary", "arbitrary"),
            vmem_limit_bytes=V7X_VMEM_LIMIT),
        name="mla_attn",
    )(ka, qa, va)


def _diff_kernel(k_ref, q_ref, v_ref, lq1_ref, lk1_ref, lq2_ref, lk2_ref, subln_ref, o_ref, *,
                 lambda_init):
    lam = (jnp.exp(jnp.sum(lq1_ref[...] * lk1_ref[...], axis=-1, keepdims=True))
           - jnp.exp(jnp.sum(lq2_ref[...] * lk2_ref[...], axis=-1, keepdims=True))
           + lambda_init)
    subln = _lane_tile(subln_ref[...], DIFF_Q_TILE)
    zeros = jnp.zeros((DIFF_HD, DIFF_Q_TILE), BF16)

    def scores(hd, q0, c0, t):
        lo = hd * DIFF_V
        q12 = q_ref[0, lo:lo + DIFF_V, q0:q0 + DIFF_Q_TILE]
        k12 = k_ref[0, c0:c0 + DIFF_KEY_CHUNK, lo:lo + DIFF_V]
        if t == 0:
            q = jnp.concatenate([q12[:DIFF_HD], zeros], axis=0)
        else:
            q = jnp.concatenate([zeros, q12[DIFF_HD:]], axis=0)
        return jnp.dot(k12, q, preferred_element_type=F32)

    def values(hd, c0):
        return v_ref[0, hd * DIFF_VS:(hd + 1) * DIFF_VS, c0:c0 + DIFF_KEY_CHUNK]

    def finish(hd, q0, accs):
        o = _normalise(accs[0], DIFF_V) - lam * _normalise(accs[1], DIFF_V)
        r = lax.rsqrt(jnp.mean(o * o, axis=0, keepdims=True) + NORM_EPS)
        o_ref[0, hd * DIFF_V:(hd + 1) * DIFF_V, q0:q0 + DIFF_Q_TILE] = (
            (o * r * subln) * (1.0 - lambda_init)).astype(BF16)

    _attention_pipeline(DIFF_HEADS_PER_STEP, k_ref.shape[1], DIFF_Q_TILE, DIFF_KEY_CHUNK, 2,
                        scores, values, finish)


def _diff_attention(kb, qb, vb, lq1, lk1, lq2, lk2, subln, lambda_init):
    bsz, s, _ = kb.shape
    hp = DIFF_HEADS_PER_STEP
    vec = _const_spec((1, DIFF_HD))
    return pl.pallas_call(
        functools.partial(_diff_kernel, lambda_init=lambda_init),
        grid=(bsz, DIFF_HEADS // hp),
        in_specs=[
            pl.BlockSpec((1, s, hp * DIFF_V), lambda b, h: (b, 0, h)),
            pl.BlockSpec((1, hp * DIFF_V, s), lambda b, h: (b, h, 0)),
            pl.BlockSpec((1, hp * DIFF_VS, s), lambda b, h: (b, h, 0)),
            vec, vec, vec, vec,
            _const_spec((DIFF_V, LANES)),
        ],
        out_specs=pl.BlockSpec((1, hp * DIFF_V, s), lambda b, h: (b, h, 0)),
        out_shape=jax.ShapeDtypeStruct((bsz, DIFF_HEADS * DIFF_V, s), BF16),
        compiler_params=pltpu.CompilerParams(
            dimension_semantics=("arbitrary", "arbitrary"),
            vmem_limit_bytes=V7X_VMEM_LIMIT),
        name="diff_attn",
    )(kb, qb, vb, lq1, lk1, lq2, lk2, subln)


def _merge_kernel(oa_ref, ob_ref, g_ref, h_ref, mod_ref, woa_ref, wob_ref, wout_ref, o_ref):
    d = h_ref.shape[2]
    gate = mod_ref[0, 5:6, :]
    groups = [slice(t0, t0 + MERGE_SUB_TILE) for t0 in range(0, h_ref.shape[1], MERGE_SUB_TILE)]

    def branch_mix(tok):
        ya = jnp.dot(woa_ref[...], oa_ref[0, :, tok], preferred_element_type=F32)
        yb = jnp.dot(wob_ref[...], ob_ref[0, :, tok], preferred_element_type=F32)
        ga = g_ref[0, :d, tok].astype(F32)
        gb = g_ref[0, d:, tok].astype(F32)
        return (ga * ya + gb * yb).astype(BF16)

    def project(tok, mix):
        y_t = jnp.dot(wout_ref[...], mix, preferred_element_type=F32)
        o_ref[0, tok, :] = h_ref[0, tok, :] + gate * y_t.T

    mixes = [branch_mix(groups[0])]
    for prev, cur in zip(groups[:-1], groups[1:]):
        mixes.append(branch_mix(cur))
        project(prev, mixes[-2])
    project(groups[-1], mixes[-1])


def _merge(oa, ob, g, h, mod, woa_t, wob_t, wout_t):
    bsz, s, d = h.shape
    tm = MERGE_TOKEN_TILE
    return pl.pallas_call(
        _merge_kernel,
        grid=(bsz, s // tm),
        in_specs=[
            pl.BlockSpec((1, oa.shape[1], tm), lambda b, i: (b, 0, i)),
            pl.BlockSpec((1, ob.shape[1], tm), lambda b, i: (b, 0, i)),
            pl.BlockSpec((1, g.shape[1], tm), lambda b, i: (b, 0, i)),
            pl.BlockSpec((1, tm, d), lambda b, i: (b, i, 0)),
            pl.BlockSpec((1, N_MOD, d), lambda b, i: (b, 0, 0)),
            _const_spec(woa_t.shape),
            _const_spec(wob_t.shape),
            _const_spec(wout_t.shape),
        ],
        out_specs=pl.BlockSpec((1, tm, d), lambda b, i: (b, i, 0)),
        out_shape=jax.ShapeDtypeStruct((bsz, s, d), F32),
        compiler_params=pltpu.CompilerParams(
            dimension_semantics=("arbitrary", "arbitrary"),
            vmem_limit_bytes=V7X_VMEM_LIMIT),
        name="merge",
    )(oa, ob, g, h, mod, woa_t, wob_t, wout_t)


def _lane_bcast(v, rows=None):
    n = v.shape[0]
    out = jnp.broadcast_to(v.astype(F32)[:, None], (n, LANES))
    if rows is not None and rows > n:
        out = jnp.pad(out, ((0, rows - n), (0, 0)))
    return out


def _rope_freqs(half, theta):
    return 1.0 / (theta ** (jnp.arange(half, dtype=F32) / half))


def _prep_table(vectors):
    parts = []
    for name, (_, rows) in PREP_TABLE.items():
        v = vectors[name].astype(F32)
        parts.append(jnp.pad(v, (0, rows - v.shape[0])))
    return jnp.concatenate(parts)


def _layer_weights(l, w_in, mla_q_norm, mla_w_uq, mla_kv_norm, mla_w_ukv, mla_q_gain, mla_k_gain,
                   diff_q_gain, diff_k_gain):
    wuq = mla_w_uq[l].reshape(MLA_Q_LORA, MLA_HEADS, MLA_QK)
    wuq = jnp.pad(wuq, ((0, 0), (0, 0), (0, HEAD_PAD - MLA_QK))).reshape(MLA_Q_LORA, -1)
    return {
        "win_t": w_in[l].T.astype(BF16),
        "wuq_t": wuq.T.astype(BF16),
        "wukv_t": mla_w_ukv[l].T.astype(BF16),
        "table": _lane_bcast(_prep_table({
            "qnorm": mla_q_norm[l], "kvnorm": mla_kv_norm[l],
            "qgain_a": mla_q_gain[l], "kgain_a": mla_k_gain[l],
            "qgain_b": diff_q_gain[l], "kgain_b": diff_k_gain[l],
            "freq_a": _rope_freqs(MLA_ROPE // 2, MLA_THETA),
            "freq_b": _rope_freqs(DIFF_ROT // 2, DIFF_THETA)})),
    }


def kernel(x, c, positions, w_ada, b_ada, ffn1_norm, ffn1_w_gate, ffn1_w_up, ffn1_w_down, mix_norm, w_in, mla_q_norm, mla_w_uq, mla_kv_norm, mla_w_ukv, mla_q_gain, mla_k_gain, mla_w_o, diff_q_gain, diff_k_gain, diff_lambda_q1, diff_lambda_k1, diff_lambda_q2, diff_lambda_k2, diff_subln, diff_w_o, w_out, ffn2_norm, ffn2_w_gate, ffn2_w_up, ffn2_w_down, final_norm):
    bsz, s, d = x.shape
    depth = w_ada.shape[0]
    pos3 = positions.reshape(bsz, 1, s)
    h = x
    for l in range(depth):
        lambda_init = 0.8 - 0.6 * math.exp(-0.3 * l)
        mod = _ada(c, w_ada[l], b_ada[l]).reshape(bsz, N_MOD, d)

        h, n_mix = _ffn(h, mod, ffn1_norm[l], ffn1_w_gate[l], ffn1_w_up[l], ffn1_w_down[l],
                        sub=0, next_gain=mix_norm[l])

        w = _layer_weights(l, w_in, mla_q_norm, mla_w_uq, mla_kv_norm, mla_w_ukv, mla_q_gain,
                           mla_k_gain, diff_q_gain, diff_k_gain)
        qa, ka, va, qb, kb, vb, g = _prep(n_mix, pos3, w)
        oa = _mla_attention(ka, qa, va)
        ob = _diff_attention(kb, qb, vb,
                             diff_lambda_q1[l].reshape(1, -1), diff_lambda_k1[l].reshape(1, -1),
                             diff_lambda_q2[l].reshape(1, -1), diff_lambda_k2[l].reshape(1, -1),
                             _lane_bcast(diff_subln[l]), lambda_init)
        h = _merge(oa, ob, g, h, mod, mla_w_o[l].T.astype(BF16), diff_w_o[l].T.astype(BF16),
                   w_out[l].T.astype(BF16))

        h = _ffn(h, mod, ffn2_norm[l], ffn2_w_gate[l], ffn2_w_up[l], ffn2_w_down[l],
                 sub=2, final_gain=final_norm[l])
    return h
```

```python
import functools
import math

import jax
import jax.numpy as jnp
from jax import lax
from jax.experimental import pallas as pl
from jax.experimental.pallas import tpu as pltpu

F32 = jnp.float32
BF16 = jnp.bfloat16

NORM_EPS = 1e-6
N_MOD = 9

MLA_HEADS = 8
MLA_NOPE = 64
MLA_ROPE = 32
MLA_QK = MLA_NOPE + MLA_ROPE
MLA_V = 64
MLA_Q_LORA = 384
MLA_KV_LORA = 256
MLA_THETA = 10000.0
DIFF_HEADS = 4
DIFF_HD = 64
DIFF_V = 2 * DIFF_HD
DIFF_THETA = 500000.0
DIFF_ROT = DIFF_HD // 4

LANES = 128
HEAD_PAD = 128
SUM_ROWS = 16
MLA_VS = MLA_V + SUM_ROWS
DIFF_VS = DIFF_V + SUM_ROWS
LOG2E = math.log2(math.e)
V7X_VMEM_LIMIT = 56 * 1024 * 1024

def _table_layout(sizes):
    layout, lo = {}, 0
    for name, rows in sizes:
        layout[name] = (lo, rows)
        lo += rows
    return layout


PREP_TABLE = _table_layout([
    ("qnorm", MLA_Q_LORA), ("kvnorm", MLA_KV_LORA), ("qgain_a", HEAD_PAD), ("kgain_a", HEAD_PAD),
    ("qgain_b", DIFF_HD), ("kgain_b", DIFF_HD), ("freq_a", MLA_ROPE // 2), ("freq_b", DIFF_ROT // 2)])

PREP_TOKEN_TILE = 1024
PREP_SUB_TILE = 1024
FFN_TOKEN_TILE = 1024
FFN_SUB_TILE = 512
MERGE_TOKEN_TILE = 1024
MERGE_SUB_TILE = 512
MLA_Q_TILE = 512
DIFF_Q_TILE = 512
MLA_KEY_CHUNK = 256
DIFF_KEY_CHUNK = 512
MLA_HEADS_PER_STEP = 4
DIFF_HEADS_PER_STEP = 2
FF_CHUNK = 256
LOAD_SLOTS = 4
ADA_COL_BLOCK = 2304


def _sigmoid(x):
    return 1.0 / (1.0 + jnp.exp(-x))


def _rms_rows(x, gain):
    ms = jnp.mean(x * x, axis=-1, keepdims=True)
    return x * lax.rsqrt(ms + NORM_EPS) * gain


def _lane_tile(g, width):
    return jnp.tile(g, (1, width // LANES))


def _const_spec(shape):
    return pl.BlockSpec(shape, lambda *_: (0,) * len(shape), pipeline_mode=pl.Buffered(1))


def _ada_kernel(c_ref, w_ref, b_ref, o_ref):
    c = c_ref[...]
    bsz = c.shape[0]
    cond = c * _sigmoid(c)
    c_hi = cond.astype(BF16).astype(F32)
    lhs = jnp.concatenate([c_hi, cond - c_hi], axis=0).astype(BF16)
    w = w_ref[...]
    w_hi = w.astype(BF16)
    w_lo = (w - w_hi.astype(F32)).astype(BF16)
    a = jnp.dot(lhs, w_hi, preferred_element_type=F32)
    b = jnp.dot(lhs, w_lo, preferred_element_type=F32)
    o_ref[...] = a[:bsz] + a[bsz:] + b[:bsz] + b_ref[...]


def _ada(c, w_ada, b_ada):
    bsz, d = c.shape
    cols = w_ada.shape[1]
    return pl.pallas_call(
        _ada_kernel,
        grid=(cols // ADA_COL_BLOCK,),
        in_specs=[
            pl.BlockSpec((bsz, d), lambda j: (0, 0)),
            pl.BlockSpec((d, ADA_COL_BLOCK), lambda j: (0, j)),
            pl.BlockSpec((1, ADA_COL_BLOCK), lambda j: (0, j)),
        ],
        out_specs=pl.BlockSpec((bsz, ADA_COL_BLOCK), lambda j: (0, j)),
        out_shape=jax.ShapeDtypeStruct((bsz, cols), F32),
        compiler_params=pltpu.CompilerParams(dimension_semantics=("arbitrary",),
                                             vmem_limit_bytes=V7X_VMEM_LIMIT),
        name="ada",
    )(c, w_ada, b_ada.reshape(1, cols))


def _adaln(x, gain, mod_ref, sub, dtype=BF16):
    shift = mod_ref[0, 3 * sub:3 * sub + 1, :]
    scale = mod_ref[0, 3 * sub + 1:3 * sub + 2, :]
    return (_rms_rows(x, gain) * (1.0 + scale) + shift).astype(dtype)


class _WeightStream:
    def __init__(self, wg_hbm, wu_hbm, wd_hbm, wgu_s, wd_s, stage_c, stage_r, sem):
        self.sem = sem
        self.jobs = []
        for c0 in range(0, wd_s.shape[0], FF_CHUNK):
            blk = slice(c0, c0 + FF_CHUNK)
            self.jobs.append((wg_hbm.at[:, blk], stage_c, wgu_s.at[:, 2 * c0:2 * c0 + FF_CHUNK]))
            self.jobs.append((wu_hbm.at[:, blk], stage_c,
                              wgu_s.at[:, 2 * c0 + FF_CHUNK:2 * (c0 + FF_CHUNK)]))
            self.jobs.append((wd_hbm.at[blk, :], stage_r, wd_s.at[blk, :]))
        self.landed = 0
        for i in range(min(LOAD_SLOTS - 1, len(self.jobs))):
            self._copy(i).start()

    def _copy(self, i):
        src, stage, _ = self.jobs[i]
        return pltpu.make_async_copy(src, stage.at[i % LOAD_SLOTS], self.sem.at[i % LOAD_SLOTS])

    def land(self, count):
        for i in range(self.landed, self.landed + count):
            if i + LOAD_SLOTS - 1 < len(self.jobs):
                self._copy(i + LOAD_SLOTS - 1).start()
            self._copy(i).wait()
            _, stage, dst = self.jobs[i]
            dst[...] = stage[i % LOAD_SLOTS].astype(BF16)
        self.landed += count


def _ffn_kernel(*refs, sub, emit_next, final):
    refs = list(refs)
    x_ref, mod_ref, gain_ref = refs.pop(0), refs.pop(0), refs.pop(0)
    w_hbm = [refs.pop(0) for _ in range(3)]
    next_gain_ref = refs.pop(0) if emit_next else None
    fgain_ref = refs.pop(0) if final else None
    o_ref = refs.pop(0)
    n_next_ref = refs.pop(0) if emit_next else None
    a_ref, wgu_ref, wd_ref, stage_c, stage_r, sem = refs

    d_ff = wd_ref.shape[0]
    gate = mod_ref[0, 3 * sub + 2:3 * sub + 3, :]
    halves = [slice(r0, r0 + FFN_SUB_TILE) for r0 in range(0, x_ref.shape[1], FFN_SUB_TILE)]
    blocks = range(0, d_ff, FF_CHUNK)
    first = (pl.program_id(0) == 0) & (pl.program_id(1) == 0)

    def norm(rows):
        return _adaln(x_ref[0, rows, :], gain_ref[...], mod_ref, sub)

    def up_block(n, rows, c0):
        gu = jnp.dot(n, wgu_ref[:, 2 * c0:2 * (c0 + FF_CHUNK)], preferred_element_type=F32)
        g, u = gu[:, :FF_CHUNK], gu[:, FF_CHUNK:]
        a_ref[rows, c0:c0 + FF_CHUNK] = ((g * _sigmoid(g)) * u).astype(BF16)

    def up(rows):
        n = norm(rows)
        for c0 in blocks:
            up_block(n, rows, c0)

    def down(rows):
        f = jnp.dot(a_ref[rows, :], wd_ref[...], preferred_element_type=F32)
        h = x_ref[0, rows, :] + (0.5 * gate) * f
        if final:
            h = _rms_rows(h, fgain_ref[...])
        o_ref[0, rows, :] = h
        if emit_next:
            n_next_ref[0, :, rows] = _adaln(h, next_gain_ref[...], mod_ref, sub + 1, F32).T.astype(BF16)

    @pl.when(first)
    def _():
        stream = _WeightStream(*w_hbm, wgu_ref, wd_ref, stage_c, stage_r, sem)
        norms = [norm(rows) for rows in halves]
        for c0 in blocks:
            stream.land(3)
            for n, rows in zip(norms, halves):
                up_block(n, rows, c0)
        for rows in halves:
            down(rows)

    @pl.when(jnp.logical_not(first))
    def _():
        up(halves[0])
        for prev, cur in zip(halves[:-1], halves[1:]):
            up(cur)
            down(prev)
        down(halves[-1])


def _ffn(h, mod, gain, wg, wu, wd, *, sub, next_gain=None, final_gain=None):
    bsz, s, d = h.shape
    d_ff = wg.shape[1]
    tm = FFN_TOKEN_TILE
    tile = pl.BlockSpec((1, tm, d), lambda b, i: (b, i, 0))
    hbm = pl.BlockSpec(memory_space=pl.ANY)
    in_specs = [tile, pl.BlockSpec((1, N_MOD, d), lambda b, i: (b, 0, 0)), _const_spec((1, d)),
                hbm, hbm, hbm]
    args = [h, mod, gain.reshape(1, d), wg, wu, wd]
    for extra in (next_gain, final_gain):
        if extra is not None:
            in_specs.append(_const_spec((1, d)))
            args.append(extra.reshape(1, d))
    out_specs, out_shape = [tile], [jax.ShapeDtypeStruct((bsz, s, d), F32)]
    if next_gain is not None:
        out_specs.append(pl.BlockSpec((1, d, tm), lambda b, i: (b, 0, i)))
        out_shape.append(jax.ShapeDtypeStruct((bsz, d, s), BF16))
    outs = pl.pallas_call(
        functools.partial(_ffn_kernel, sub=sub, emit_next=next_gain is not None,
                          final=final_gain is not None),
        grid=(bsz, s // tm),
        in_specs=in_specs,
        out_specs=out_specs,
        out_shape=out_shape,
        scratch_shapes=[
            pltpu.VMEM((tm, d_ff), BF16),
            pltpu.VMEM((d, 2 * d_ff), BF16), pltpu.VMEM((d_ff, d), BF16),
            pltpu.VMEM((LOAD_SLOTS, d, FF_CHUNK), F32), pltpu.VMEM((LOAD_SLOTS, FF_CHUNK, d), F32),
            pltpu.SemaphoreType.DMA((LOAD_SLOTS,)),
        ],
        compiler_params=pltpu.CompilerParams(
            dimension_semantics=("arbitrary", "arbitrary"),
            vmem_limit_bytes=V7X_VMEM_LIMIT),
        name="ffn%d" % sub,
    )(*args)
    return outs if next_gain is not None else outs[0]


def _rope_rows(x1, x2, cos, sin):
    return x1 * cos - x2 * sin, x2 * cos + x1 * sin


def _prep_kernel(*refs):
    for t0 in range(0, refs[0].shape[2], PREP_SUB_TILE):
        _prep_group(slice(t0, t0 + PREP_SUB_TILE), *refs)


def _prep_group(tok, n_ref, pos_ref, win_ref, wuq_ref, wukv_ref,
                tab_ref,
                qa_ref, ka_ref, va_ref, qb_ref, kb_ref, vb_ref, g_ref):
    tm = PREP_SUB_TILE
    n_t = n_ref[0, :, tok]

    def row_consts(name):
        lo, rows = PREP_TABLE[name]
        return _lane_tile(tab_ref[lo:lo + rows, :], tm)

    def proj_t(r0, r1):
        return jnp.dot(win_ref[r0:r1, :], n_t, preferred_element_type=F32)

    o_q, o_kv, o_kr = 0, MLA_Q_LORA, MLA_Q_LORA + MLA_KV_LORA
    o_qb = o_kr + MLA_ROPE
    w_b = DIFF_HEADS * DIFF_V
    o_kb, o_vb, o_g = o_qb + w_b, o_qb + 2 * w_b, o_qb + 3 * w_b
    g_chunk = g_ref.shape[1] // 4

    def emit_gates(i):
        r0 = i * g_chunk
        z = proj_t(o_g + r0, o_g + r0 + g_chunk)
        g_ref[0, r0:r0 + g_chunk, tok] = _sigmoid(z).astype(BF16)

    z_a = proj_t(o_q, o_qb)
    emit_gates(0)
    zq, zkv, kr = z_a[o_q:o_kv], z_a[o_kv:o_kr], z_a[o_kr:o_qb]
    rq = lax.rsqrt(jnp.mean(zq * zq, axis=0, keepdims=True) + NORM_EPS)
    zqn = (zq * rq * row_consts("qnorm")).astype(BF16)
    q_all = jnp.dot(wuq_ref[...], zqn, preferred_element_type=F32)
    rkv = lax.rsqrt(jnp.mean(zkv * zkv, axis=0, keepdims=True) + NORM_EPS)
    zkvn = (zkv * rkv * row_consts("kvnorm")).astype(BF16)
    kv_all = jnp.dot(wukv_ref[...], zkvn, preferred_element_type=F32)
    emit_gates(1)

    pos = pos_ref[0, :, tok].astype(F32)
    ang_a = pos * row_consts("freq_a")
    cos_a, sin_a = jnp.cos(ang_a), jnp.sin(ang_a)
    ang_b = pos * row_consts("freq_b")
    cos_b, sin_b = jnp.cos(ang_b), jnp.sin(ang_b)

    qgain_a = row_consts("qgain_a")
    kgain_a = row_consts("kgain_a")
    q_scale = LOG2E / math.sqrt(MLA_QK)
    pad_rows = jnp.zeros((HEAD_PAD - MLA_QK, tm), F32)
    sum_rows = (lax.broadcasted_iota(jnp.int32, (SUM_ROWS, tm), 0) == 0).astype(BF16)
    half = MLA_ROPE // 2

    def head_norm_rope_a(xh, gain):
        r = lax.rsqrt(jnp.sum(xh * xh, axis=0, keepdims=True) * (1.0 / MLA_QK) + NORM_EPS)
        xh = xh * r * gain
        r1, r2 = _rope_rows(xh[MLA_NOPE:MLA_NOPE + half], xh[MLA_NOPE + half:MLA_QK], cos_a, sin_a)
        return jnp.concatenate([xh[:MLA_NOPE], r1, r2, xh[MLA_QK:]], axis=0)

    zqb = proj_t(o_qb, o_kb)
    zkb = proj_t(o_kb, o_vb)
    for hd in range(MLA_HEADS):
        lo = hd * HEAD_PAD
        qh = head_norm_rope_a(q_all[lo:lo + HEAD_PAD], qgain_a) * q_scale
        qa_ref[0, lo:lo + HEAD_PAD, tok] = qh.astype(BF16)
        kvh = kv_all[lo:lo + HEAD_PAD]
        va_ref[0, hd * MLA_VS:hd * MLA_VS + MLA_V, tok] = kvh[MLA_NOPE:].astype(BF16)
        va_ref[0, hd * MLA_VS + MLA_V:(hd + 1) * MLA_VS, tok] = sum_rows
        kh = jnp.concatenate([kvh[:MLA_NOPE], kr, pad_rows], axis=0)
        kh = head_norm_rope_a(kh, kgain_a)
        ka_ref[0, tok, lo:lo + HEAD_PAD] = kh.T.astype(BF16)
    emit_gates(2)

    qgain_b = row_consts("qgain_b")
    kgain_b = row_consts("kgain_b")
    qb_scale = LOG2E / math.sqrt(DIFF_HD)
    hb = DIFF_ROT // 2

    def head_norm_rope_b(xh, gain):
        r = lax.rsqrt(jnp.mean(xh * xh, axis=0, keepdims=True) + NORM_EPS)
        xh = xh * r * gain
        r1, r2 = _rope_rows(xh[:hb], xh[hb:DIFF_ROT], cos_b, sin_b)
        return jnp.concatenate([r1, r2, xh[DIFF_ROT:]], axis=0)

    zvb = proj_t(o_vb, o_g)
    k_parts = []
    for blk in range(2 * DIFF_HEADS):
        lo = blk * DIFF_HD
        qh = head_norm_rope_b(zqb[lo:lo + DIFF_HD], qgain_b) * qb_scale
        qb_ref[0, lo:lo + DIFF_HD, tok] = qh.astype(BF16)
        k_parts.append(head_norm_rope_b(zkb[lo:lo + DIFF_HD], kgain_b))
    for hd in range(DIFF_HEADS):
        k12 = jnp.concatenate(k_parts[2 * hd:2 * hd + 2], axis=0)
        kb_ref[0, tok, hd * DIFF_V:(hd + 1) * DIFF_V] = k12.T.astype(BF16)
        vb_ref[0, hd * DIFF_VS:hd * DIFF_VS + DIFF_V, tok] = zvb[hd * DIFF_V:(hd + 1) * DIFF_V].astype(BF16)
        vb_ref[0, hd * DIFF_VS + DIFF_V:(hd + 1) * DIFF_VS, tok] = sum_rows
    emit_gates(3)


def _prep(n_t, pos3, w):
    bsz, d, s = n_t.shape
    tm = PREP_TOKEN_TILE
    n_in = w["win_t"].shape[0]
    wa, wb = MLA_HEADS * HEAD_PAD, DIFF_HEADS * DIFF_V
    n_gate = n_in - (MLA_Q_LORA + MLA_KV_LORA + MLA_ROPE + 3 * wb)

    def fm(rows):
        return (jax.ShapeDtypeStruct((bsz, rows, s), BF16),
                pl.BlockSpec((1, rows, tm), lambda b, i: (b, 0, i)))

    def tmaj(cols):
        return (jax.ShapeDtypeStruct((bsz, s, cols), BF16),
                pl.BlockSpec((1, tm, cols), lambda b, i: (b, i, 0)))

    outs = [fm(wa), tmaj(wa), fm(MLA_HEADS * MLA_VS), fm(wb), tmaj(wb), fm(DIFF_HEADS * DIFF_VS), fm(n_gate)]
    consts = [w["win_t"], w["wuq_t"], w["wukv_t"], w["table"]]
    in_specs = [
        pl.BlockSpec((1, d, tm), lambda b, i: (b, 0, i)),
        pl.BlockSpec((1, 1, tm), lambda b, i: (b, 0, i)),
    ] + [_const_spec(a.shape) for a in consts]
    return pl.pallas_call(
        _prep_kernel,
        grid=(bsz, s // tm),
        in_specs=in_specs,
        out_specs=[o[1] for o in outs],
        out_shape=[o[0] for o in outs],
        compiler_params=pltpu.CompilerParams(
            dimension_semantics=("arbitrary", "arbitrary"),
            vmem_limit_bytes=V7X_VMEM_LIMIT),
        name="prep",
    )(n_t, pos3, *consts)


def _normalise(acc, rows):
    return acc[:rows] * (1.0 / acc[rows:rows + 1])


def _attention_pipeline(n_heads, s_len, tq, kc, n_streams, scores_fn, v_fn, finish_fn):
    units = [(hd, q0, c0) for hd in range(n_heads) for q0 in range(0, s_len, tq)
             for c0 in range(0, s_len, kc)]
    streams = range(n_streams)
    n = len(units)
    s_cur = [scores_fn(*units[0], t) for t in streams]
    m = [None] * n_streams
    acc = [None] * n_streams
    pending = None
    for i in range(n + 1):
        s_nxt = [scores_fn(*units[i + 1], t) for t in streams] if i + 1 < n else None
        if pending is not None:
            (hd, q0, c0), alphas, probs = pending
            v = v_fn(hd, c0)
            for t in streams:
                o = jnp.dot(v, probs[t], preferred_element_type=F32)
                acc[t] = o if alphas[t] is None else alphas[t] * acc[t] + o
            if c0 + kc == s_len:
                finish_fn(hd, q0, acc)
                acc = [None] * n_streams
            pending = None
        if i < n:
            _, _, c0 = units[i]
            alphas, probs = [], []
            for t in streams:
                cmax = jnp.max(s_cur[t], axis=0, keepdims=True)
                if c0 == 0:
                    m[t] = cmax
                    alphas.append(None)
                else:
                    m_new = jnp.maximum(m[t], cmax)
                    alphas.append(jnp.exp2(m[t] - m_new))
                    m[t] = m_new
                probs.append(jnp.exp2(s_cur[t] - m[t]).astype(BF16))
            pending = (units[i], alphas, probs)
            s_cur = s_nxt


def _mla_kernel(k_ref, q_ref, v_ref, o_ref):
    def scores(hd, q0, c0, t):
        return jnp.dot(k_ref[0, c0:c0 + MLA_KEY_CHUNK, hd * HEAD_PAD:(hd + 1) * HEAD_PAD],
                       q_ref[0, hd * HEAD_PAD:(hd + 1) * HEAD_PAD, q0:q0 + MLA_Q_TILE],
                       preferred_element_type=F32)

    def values(hd, c0):
        return v_ref[0, hd * MLA_VS:(hd + 1) * MLA_VS, c0:c0 + MLA_KEY_CHUNK]

    def finish(hd, q0, accs):
        o_ref[0, hd * MLA_V:(hd + 1) * MLA_V, q0:q0 + MLA_Q_TILE] = (
            _normalise(accs[0], MLA_V).astype(BF16))

    _attention_pipeline(MLA_HEADS_PER_STEP, k_ref.shape[1], MLA_Q_TILE, MLA_KEY_CHUNK, 1,
                        scores, values, finish)


def _mla_attention(ka, qa, va):
    bsz, s, _ = ka.shape
    hp = MLA_HEADS_PER_STEP
    return pl.pallas_call(
        _mla_kernel,
        grid=(bsz, MLA_HEADS // hp),
        in_specs=[
            pl.BlockSpec((1, s, hp * HEAD_PAD), lambda b, h: (b, 0, h)),
            pl.BlockSpec((1, hp * HEAD_PAD, s), lambda b, h: (b, h, 0)),
            pl.BlockSpec((1, hp * MLA_VS, s), lambda b, h: (b, h, 0)),
        ],
        out_specs=pl.BlockSpec((1, hp * MLA_V, s), lambda b, h: (b, h, 0)),
        out_shape=jax.ShapeDtypeStruct((bsz, MLA_HEADS * MLA_V, s), BF16),
        compiler_params=pltpu.CompilerParams(
            dimension_semantics=("arbitrary", "arbitrary"),
            vmem_limit_bytes=V7X_VMEM_LIMIT),
        name="mla_attn",
    )(ka, qa, va)


def _diff_kernel(k_ref, q_ref, v_ref, lq1_ref, lk1_ref, lq2_ref, lk2_ref, subln_ref, o_ref, *,
                 lambda_init):
    lam = (jnp.exp(jnp.sum(lq1_ref[...] * lk1_ref[...], axis=-1, keepdims=True))
           - jnp.exp(jnp.sum(lq2_ref[...] * lk2_ref[...], axis=-1, keepdims=True))
           + lambda_init)
    subln = _lane_tile(subln_ref[...], DIFF_Q_TILE)
    zeros = jnp.zeros((DIFF_HD, DIFF_Q_TILE), BF16)

    def scores(hd, q0, c0, t):
        lo = hd * DIFF_V
        q12 = q_ref[0, lo:lo + DIFF_V, q0:q0 + DIFF_Q_TILE]
        k12 = k_ref[0, c0:c0 + DIFF_KEY_CHUNK, lo:lo + DIFF_V]
        if t == 0:
            q = jnp.concatenate([q12[:DIFF_HD], zeros], axis=0)
        else:
            q = jnp.concatenate([zeros, q12[DIFF_HD:]], axis=0)
        return jnp.dot(k12, q, preferred_element_type=F32)

    def values(hd, c0):
        return v_ref[0, hd * DIFF_VS:(hd + 1) * DIFF_VS, c0:c0 + DIFF_KEY_CHUNK]

    def finish(hd, q0, accs):
        o = _normalise(accs[0], DIFF_V) - lam * _normalise(accs[1], DIFF_V)
        r = lax.rsqrt(jnp.mean(o * o, axis=0, keepdims=True) + NORM_EPS)
        o_ref[0, hd * DIFF_V:(hd + 1) * DIFF_V, q0:q0 + DIFF_Q_TILE] = (
            (o * r * subln) * (1.0 - lambda_init)).astype(BF16)

    _attention_pipeline(DIFF_HEADS_PER_STEP, k_ref.shape[1], DIFF_Q_TILE, DIFF_KEY_CHUNK, 2,
                        scores, values, finish)


def _diff_attention(kb, qb, vb, lq1, lk1, lq2, lk2, subln, lambda_init):
    bsz, s, _ = kb.shape
    hp = DIFF_HEADS_PER_STEP
    vec = _const_spec((1, DIFF_HD))
    return pl.pallas_call(
        functools.partial(_diff_kernel, lambda_init=lambda_init),
        grid=(bsz, DIFF_HEADS // hp),
        in_specs=[
            pl.BlockSpec((1, s, hp * DIFF_V), lambda b, h: (b, 0, h)),
            pl.BlockSpec((1, hp * DIFF_V, s), lambda b, h: (b, h, 0)),
            pl.BlockSpec((1, hp * DIFF_VS, s), lambda b, h: (b, h, 0)),
            vec, vec, vec, vec,
            _const_spec((DIFF_V, LANES)),
        ],
        out_specs=pl.BlockSpec((1, hp * DIFF_V, s), lambda b, h: (b, h, 0)),
        out_shape=jax.ShapeDtypeStruct((bsz, DIFF_HEADS * DIFF_V, s), BF16),
        compiler_params=pltpu.CompilerParams(
            dimension_semantics=("arbitrary", "arbitrary"),
            vmem_limit_bytes=V7X_VMEM_LIMIT),
        name="diff_attn",
    )(kb, qb, vb, lq1, lk1, lq2, lk2, subln)


def _merge_kernel(oa_ref, ob_ref, g_ref, h_ref, mod_ref, woa_ref, wob_ref, wout_ref, o_ref):
    d = h_ref.shape[2]
    gate = mod_ref[0, 5:6, :]
    groups = [slice(t0, t0 + MERGE_SUB_TILE) for t0 in range(0, h_ref.shape[1], MERGE_SUB_TILE)]

    def branch_mix(tok):
        ya = jnp.dot(woa_ref[...], oa_ref[0, :, tok], preferred_element_type=F32)
        yb = jnp.dot(wob_ref[...], ob_ref[0, :, tok], preferred_element_type=F32)
        ga = g_ref[0, :d, tok].astype(F32)
        gb = g_ref[0, d:, tok].astype(F32)
        return (ga * ya + gb * yb).astype(BF16)

    def project(tok, mix):
        for f0 in range(0, d, d // 2):
            cols = slice(f0, f0 + d // 2)
            y_t = jnp.dot(wout_ref[cols, :], mix, preferred_element_type=F32)
            o_ref[0, tok, cols] = h_ref[0, tok, cols] + gate[:, cols] * y_t.T

    mixes = [branch_mix(groups[0])]
    for prev, cur in zip(groups[:-1], groups[1:]):
        mixes.append(branch_mix(cur))
        project(prev, mixes[-2])
    project(groups[-1], mixes[-1])


def _merge(oa, ob, g, h, mod, woa_t, wob_t, wout_t):
    bsz, s, d = h.shape
    tm = MERGE_TOKEN_TILE
    return pl.pallas_call(
        _merge_kernel,
        grid=(bsz, s // tm),
        in_specs=[
            pl.BlockSpec((1, oa.shape[1], tm), lambda b, i: (b, 0, i)),
            pl.BlockSpec((1, ob.shape[1], tm), lambda b, i: (b, 0, i)),
            pl.BlockSpec((1, g.shape[1], tm), lambda b, i: (b, 0, i)),
            pl.BlockSpec((1, tm, d), lambda b, i: (b, i, 0)),
            pl.BlockSpec((1, N_MOD, d), lambda b, i: (b, 0, 0)),
            _const_spec(woa_t.shape),
            _const_spec(wob_t.shape),
            _const_spec(wout_t.shape),
        ],
        out_specs=pl.BlockSpec((1, tm, d), lambda b, i: (b, i, 0)),
        out_shape=jax.ShapeDtypeStruct((bsz, s, d), F32),
        compiler_params=pltpu.CompilerParams(
            dimension_semantics=("arbitrary", "arbitrary"),
            vmem_limit_bytes=V7X_VMEM_LIMIT),
        name="merge",
    )(oa, ob, g, h, mod, woa_t, wob_t, wout_t)


def _lane_bcast(v, rows=None):
    n = v.shape[0]
    out = jnp.broadcast_to(v.astype(F32)[:, None], (n, LANES))
    if rows is not None and rows > n:
        out = jnp.pad(out, ((0, rows - n), (0, 0)))
    return out


def _rope_freqs(half, theta):
    return 1.0 / (theta ** (jnp.arange(half, dtype=F32) / half))


def _prep_table(vectors):
    parts = []
    for name, (_, rows) in PREP_TABLE.items():
        v = vectors[name].astype(F32)
        parts.append(jnp.pad(v, (0, rows - v.shape[0])))
    return jnp.concatenate(parts)


def _layer_weights(l, w_in, mla_q_norm, mla_w_uq, mla_kv_norm, mla_w_ukv, mla_q_gain, mla_k_gain,
                   diff_q_gain, diff_k_gain):
    wuq = mla_w_uq[l].reshape(MLA_Q_LORA, MLA_HEADS, MLA_QK)
    wuq = jnp.pad(wuq, ((0, 0), (0, 0), (0, HEAD_PAD - MLA_QK))).reshape(MLA_Q_LORA, -1)
    return {
        "win_t": w_in[l].T.astype(BF16),
        "wuq_t": wuq.T.astype(BF16),
        "wukv_t": mla_w_ukv[l].T.astype(BF16),
        "table": _lane_bcast(_prep_table({
            "qnorm": mla_q_norm[l], "kvnorm": mla_kv_norm[l],
            "qgain_a": mla_q_gain[l], "kgain_a": mla_k_gain[l],
            "qgain_b": diff_q_gain[l], "kgain_b": diff_k_gain[l],
            "freq_a": _rope_freqs(MLA_ROPE // 2, MLA_THETA),
            "freq_b": _rope_freqs(DIFF_ROT // 2, DIFF_THETA)})),
    }


def kernel(x, c, positions, w_ada, b_ada, ffn1_norm, ffn1_w_gate, ffn1_w_up, ffn1_w_down, mix_norm, w_in, mla_q_norm, mla_w_uq, mla_kv_norm, mla_w_ukv, mla_q_gain, mla_k_gain, mla_w_o, diff_q_gain, diff_k_gain, diff_lambda_q1, diff_lambda_k1, diff_lambda_q2, diff_lambda_k2, diff_subln, diff_w_o, w_out, ffn2_norm, ffn2_w_gate, ffn2_w_up, ffn2_w_down, final_norm):
    bsz, s, d = x.shape
    depth = w_ada.shape[0]
    pos3 = positions.reshape(bsz, 1, s)
    h = x
    for l in range(depth):
        lambda_init = 0.8 - 0.6 * math.exp(-0.3 * l)
        mod = _ada(c, w_ada[l], b_ada[l]).reshape(bsz, N_MOD, d)

        h, n_mix = _ffn(h, mod, ffn1_norm[l], ffn1_w_gate[l], ffn1_w_up[l], ffn1_w_down[l],
                        sub=0, next_gain=mix_norm[l])

        w = _layer_weights(l, w_in, mla_q_norm, mla_w_uq, mla_kv_norm, mla_w_ukv, mla_q_gain,
                           mla_k_gain, diff_q_gain, diff_k_gain)
        qa, ka, va, qb, kb, vb, g = _prep(n_mix, pos3, w)
        oa = _mla_attention(ka, qa, va)
        ob = _diff_attention(kb, qb, vb,
                             diff_lambda_q1[l].reshape(1, -1), diff_lambda_k1[l].reshape(1, -1),
                             diff_lambda_q2[l].reshape(1, -1), diff_lambda_k2[l].reshape(1, -1),
                             _lane_bcast(diff_subln[l]), lambda_init)
        h = _merge(oa, ob, g, h, mod, mla_w_o[l].T.astype(BF16), diff_w_o[l].T.astype(BF16),
                   w_out[l].T.astype(BF16))

        h = _ffn(h, mod, ffn2_norm[l], ffn2_w_gate[l], ffn2_w_up[l], ffn2_w_down[l],
                 sub=2, final_gain=final_norm[l])
    return h
```

```python
import functools
import math

import jax
import jax.numpy as jnp
from jax import lax
from jax.experimental import pallas as pl
from jax.experimental.pallas import tpu as pltpu

F32 = jnp.float32
BF16 = jnp.bfloat16

NORM_EPS = 1e-6
N_MOD = 9

MLA_HEADS = 8
MLA_NOPE = 64
MLA_ROPE = 32
MLA_QK = MLA_NOPE + MLA_ROPE
MLA_V = 64
MLA_Q_LORA = 384
MLA_KV_LORA = 256
MLA_THETA = 10000.0
DIFF_HEADS = 4
DIFF_HD = 64
DIFF_V = 2 * DIFF_HD
DIFF_THETA = 500000.0
DIFF_ROT = DIFF_HD // 4

LANES = 128
HEAD_PAD = 128
SUM_ROWS = 16
MLA_VS = MLA_V + SUM_ROWS
DIFF_VS = DIFF_V + SUM_ROWS
LOG2E = math.log2(math.e)
V7X_VMEM_LIMIT = 56 * 1024 * 1024

def _table_layout(sizes):
    layout, lo = {}, 0
    for name, rows in sizes:
        layout[name] = (lo, rows)
        lo += rows
    return layout


PREP_TABLE = _table_layout([
    ("qnorm", MLA_Q_LORA), ("kvnorm", MLA_KV_LORA), ("qgain_a", HEAD_PAD), ("kgain_a", HEAD_PAD),
    ("qgain_b", DIFF_HD), ("kgain_b", DIFF_HD), ("freq_a", MLA_ROPE // 2), ("freq_b", DIFF_ROT // 2)])

PREP_TOKEN_TILE = 1024
PREP_SUB_TILE = 1024
FFN_TOKEN_TILE = 1024
FFN_SUB_TILE = 512
MERGE_TOKEN_TILE = 1024
MERGE_SUB_TILE = 512
MLA_Q_TILE = 512
DIFF_Q_TILE = 512
MLA_KEY_CHUNK = 256
DIFF_KEY_CHUNK = 512
MLA_HEADS_PER_STEP = 4
DIFF_HEADS_PER_STEP = 2
FF_CHUNK = 256
LOAD_SLOTS = 4
ADA_COL_BLOCK = 2304


def _sigmoid(x):
    return 1.0 / (1.0 + jnp.exp(-x))


def _rms_rows(x, gain):
    ms = jnp.mean(x * x, axis=-1, keepdims=True)
    return x * lax.rsqrt(ms + NORM_EPS) * gain


def _lane_tile(g, width):
    return jnp.tile(g, (1, width // LANES))


def _const_spec(shape):
    return pl.BlockSpec(shape, lambda *_: (0,) * len(shape), pipeline_mode=pl.Buffered(1))


def _ada_kernel(c_ref, w_ref, b_ref, o_ref):
    c = c_ref[...]
    bsz = c.shape[0]
    cond = c * _sigmoid(c)
    c_hi = cond.astype(BF16).astype(F32)
    lhs = jnp.concatenate([c_hi, cond - c_hi], axis=0).astype(BF16)
    w = w_ref[...]
    w_hi = w.astype(BF16)
    w_lo = (w - w_hi.astype(F32)).astype(BF16)
    a = jnp.dot(lhs, w_hi, preferred_element_type=F32)
    b = jnp.dot(lhs, w_lo, preferred_element_type=F32)
    o_ref[...] = a[:bsz] + a[bsz:] + b[:bsz] + b_ref[...]


def _ada(c, w_ada, b_ada):
    bsz, d = c.shape
    cols = w_ada.shape[1]
    return pl.pallas_call(
        _ada_kernel,
        grid=(cols // ADA_COL_BLOCK,),
        in_specs=[
            pl.BlockSpec((bsz, d), lambda j: (0, 0)),
            pl.BlockSpec((d, ADA_COL_BLOCK), lambda j: (0, j)),
            pl.BlockSpec((1, ADA_COL_BLOCK), lambda j: (0, j)),
        ],
        out_specs=pl.BlockSpec((bsz, ADA_COL_BLOCK), lambda j: (0, j)),
        out_shape=jax.ShapeDtypeStruct((bsz, cols), F32),
        compiler_params=pltpu.CompilerParams(dimension_semantics=("arbitrary",),
                                             vmem_limit_bytes=V7X_VMEM_LIMIT),
        name="ada",
    )(c, w_ada, b_ada.reshape(1, cols))


def _adaln(x, gain, mod_ref, sub, dtype=BF16):
    shift = mod_ref[0, 3 * sub:3 * sub + 1, :]
    scale = mod_ref[0, 3 * sub + 1:3 * sub + 2, :]
    return (_rms_rows(x, gain) * (1.0 + scale) + shift).astype(dtype)


class _WeightStream:
    def __init__(self, wg_hbm, wu_hbm, wd_hbm, wgu_s, wd_s, stage_c, stage_r, sem):
        self.sem = sem
        self.jobs = []
        for c0 in range(0, wd_s.shape[0], FF_CHUNK):
            blk = slice(c0, c0 + FF_CHUNK)
            self.jobs.append((wg_hbm.at[:, blk], stage_c, wgu_s.at[:, 2 * c0:2 * c0 + FF_CHUNK]))
            self.jobs.append((wu_hbm.at[:, blk], stage_c,
                              wgu_s.at[:, 2 * c0 + FF_CHUNK:2 * (c0 + FF_CHUNK)]))
            self.jobs.append((wd_hbm.at[blk, :], stage_r, wd_s.at[blk, :]))
        self.landed = 0
        for i in range(min(LOAD_SLOTS - 1, len(self.jobs))):
            self._copy(i).start()

    def _copy(self, i):
        src, stage, _ = self.jobs[i]
        return pltpu.make_async_copy(src, stage.at[i % LOAD_SLOTS], self.sem.at[i % LOAD_SLOTS])

    def land(self, count):
        for i in range(self.landed, self.landed + count):
            if i + LOAD_SLOTS - 1 < len(self.jobs):
                self._copy(i + LOAD_SLOTS - 1).start()
            self._copy(i).wait()
            _, stage, dst = self.jobs[i]
            dst[...] = stage[i % LOAD_SLOTS].astype(BF16)
        self.landed += count


def _ffn_kernel(*refs, sub, emit_next, final):
    refs = list(refs)
    x_ref, mod_ref, gain_ref = refs.pop(0), refs.pop(0), refs.pop(0)
    w_hbm = [refs.pop(0) for _ in range(3)]
    next_gain_ref = refs.pop(0) if emit_next else None
    fgain_ref = refs.pop(0) if final else None
    o_ref = refs.pop(0)
    n_next_ref = refs.pop(0) if emit_next else None
    a_ref, wgu_ref, wd_ref, stage_c, stage_r, sem = refs

    d_ff = wd_ref.shape[0]
    gate = mod_ref[0, 3 * sub + 2:3 * sub + 3, :]
    halves = [slice(r0, r0 + FFN_SUB_TILE) for r0 in range(0, x_ref.shape[1], FFN_SUB_TILE)]
    blocks = range(0, d_ff, FF_CHUNK)
    first = (pl.program_id(0) == 0) & (pl.program_id(1) == 0)

    def norm(rows):
        return _adaln(x_ref[0, rows, :], gain_ref[...], mod_ref, sub)

    def up_block(n, rows, c0):
        gu = jnp.dot(n, wgu_ref[:, 2 * c0:2 * (c0 + FF_CHUNK)], preferred_element_type=F32)
        g, u = gu[:, :FF_CHUNK], gu[:, FF_CHUNK:]
        a_ref[rows, c0:c0 + FF_CHUNK] = ((g * _sigmoid(g)) * u).astype(BF16)

    def up(rows):
        n = norm(rows)
        for c0 in blocks:
            up_block(n, rows, c0)

    def down(rows):
        f = jnp.dot(a_ref[rows, :], wd_ref[...], preferred_element_type=F32)
        h = x_ref[0, rows, :] + (0.5 * gate) * f
        if final:
            h = _rms_rows(h, fgain_ref[...])
        o_ref[0, rows, :] = h
        if emit_next:
            n_next_ref[0, :, rows] = _adaln(h, next_gain_ref[...], mod_ref, sub + 1, F32).T.astype(BF16)

    @pl.when(first)
    def _():
        stream = _WeightStream(*w_hbm, wgu_ref, wd_ref, stage_c, stage_r, sem)
        norms = [norm(rows) for rows in halves]
        for c0 in blocks:
            stream.land(3)
            for n, rows in zip(norms, halves):
                up_block(n, rows, c0)
        for rows in halves:
            down(rows)

    @pl.when(jnp.logical_not(first))
    def _():
        up(halves[0])
        for prev, cur in zip(halves[:-1], halves[1:]):
            up(cur)
            down(prev)
        down(halves[-1])


def _ffn(h, mod, gain, wg, wu, wd, *, sub, next_gain=None, final_gain=None):
    bsz, s, d = h.shape
    d_ff = wg.shape[1]
    tm = FFN_TOKEN_TILE
    tile = pl.BlockSpec((1, tm, d), lambda b, i: (b, i, 0))
    hbm = pl.BlockSpec(memory_space=pl.ANY)
    in_specs = [tile, pl.BlockSpec((1, N_MOD, d), lambda b, i: (b, 0, 0)), _const_spec((1, d)),
                hbm, hbm, hbm]
    args = [h, mod, gain.reshape(1, d), wg, wu, wd]
    for extra in (next_gain, final_gain):
        if extra is not None:
            in_specs.append(_const_spec((1, d)))
            args.append(extra.reshape(1, d))
    out_specs, out_shape = [tile], [jax.ShapeDtypeStruct((bsz, s, d), F32)]
    if next_gain is not None:
        out_specs.append(pl.BlockSpec((1, d, tm), lambda b, i: (b, 0, i)))
        out_shape.append(jax.ShapeDtypeStruct((bsz, d, s), BF16))
    outs = pl.pallas_call(
        functools.partial(_ffn_kernel, sub=sub, emit_next=next_gain is not None,
                          final=final_gain is not None),
        grid=(bsz, s // tm),
        in_specs=in_specs,
        out_specs=out_specs,
        out_shape=out_shape,
        scratch_shapes=[
            pltpu.VMEM((tm, d_ff), BF16),
            pltpu.VMEM((d, 2 * d_ff), BF16), pltpu.VMEM((d_ff, d), BF16),
            pltpu.VMEM((LOAD_SLOTS, d, FF_CHUNK), F32), pltpu.VMEM((LOAD_SLOTS, FF_CHUNK, d), F32),
            pltpu.SemaphoreType.DMA((LOAD_SLOTS,)),
        ],
        compiler_params=pltpu.CompilerParams(
            dimension_semantics=("arbitrary", "arbitrary"),
            vmem_limit_bytes=V7X_VMEM_LIMIT),
        name="ffn%d" % sub,
    )(*args)
    return outs if next_gain is not None else outs[0]


def _rope_rows(x1, x2, cos, sin):
    return x1 * cos - x2 * sin, x2 * cos + x1 * sin


def _prep_kernel(*refs):
    for t0 in range(0, refs[0].shape[2], PREP_SUB_TILE):
        _prep_group(slice(t0, t0 + PREP_SUB_TILE), *refs)


def _prep_group(tok, n_ref, pos_ref, win_ref, wuq_ref, wukv_ref,
                tab_ref,
                qa_ref, ka_ref, va_ref, qb_ref, kb_ref, vb_ref, g_ref):
    tm = PREP_SUB_TILE
    n_t = n_ref[0, :, tok]

    def row_consts(name):
        lo, rows = PREP_TABLE[name]
        return _lane_tile(tab_ref[lo:lo + rows, :], tm)

    def proj_t(r0, r1):
        return jnp.dot(win_ref[r0:r1, :], n_t, preferred_element_type=F32)

    o_q, o_kv, o_kr = 0, MLA_Q_LORA, MLA_Q_LORA + MLA_KV_LORA
    o_qb = o_kr + MLA_ROPE
    w_b = DIFF_HEADS * DIFF_V
    o_kb, o_vb, o_g = o_qb + w_b, o_qb + 2 * w_b, o_qb + 3 * w_b
    g_chunk = g_ref.shape[1] // 8

    def emit_gates(i):
        r0 = i * g_chunk
        z = proj_t(o_g + r0, o_g + r0 + g_chunk)
        g_ref[0, r0:r0 + g_chunk, tok] = _sigmoid(z).astype(BF16)

    z_a = proj_t(o_q, o_qb)
    emit_gates(0)
    emit_gates(1)
    zq, zkv, kr = z_a[o_q:o_kv], z_a[o_kv:o_kr], z_a[o_kr:o_qb]
    rq = lax.rsqrt(jnp.mean(zq * zq, axis=0, keepdims=True) + NORM_EPS)
    zqn = (zq * rq * row_consts("qnorm")).astype(BF16)
    q_all = jnp.dot(wuq_ref[...], zqn, preferred_element_type=F32)
    rkv = lax.rsqrt(jnp.mean(zkv * zkv, axis=0, keepdims=True) + NORM_EPS)
    zkvn = (zkv * rkv * row_consts("kvnorm")).astype(BF16)
    kv_all = jnp.dot(wukv_ref[...], zkvn, preferred_element_type=F32)
    emit_gates(2)
    emit_gates(3)

    pos = pos_ref[0, :, tok].astype(F32)
    ang_a = pos * row_consts("freq_a")
    cos_a, sin_a = jnp.cos(ang_a), jnp.sin(ang_a)
    ang_b = pos * row_consts("freq_b")
    cos_b, sin_b = jnp.cos(ang_b), jnp.sin(ang_b)

    qgain_a = row_consts("qgain_a")
    kgain_a = row_consts("kgain_a")
    q_scale = LOG2E / math.sqrt(MLA_QK)
    pad_rows = jnp.zeros((HEAD_PAD - MLA_QK, tm), F32)
    sum_rows = (lax.broadcasted_iota(jnp.int32, (SUM_ROWS, tm), 0) == 0).astype(BF16)
    half = MLA_ROPE // 2

    def head_norm_rope_a(xh, gain):
        r = lax.rsqrt(jnp.sum(xh * xh, axis=0, keepdims=True) * (1.0 / MLA_QK) + NORM_EPS)
        xh = xh * r * gain
        r1, r2 = _rope_rows(xh[MLA_NOPE:MLA_NOPE + half], xh[MLA_NOPE + half:MLA_QK], cos_a, sin_a)
        return jnp.concatenate([xh[:MLA_NOPE], r1, r2, xh[MLA_QK:]], axis=0)

    zqb = proj_t(o_qb, o_kb)
    zkb = proj_t(o_kb, o_vb)
    for hd in range(MLA_HEADS):
        lo = hd * HEAD_PAD
        qh = head_norm_rope_a(q_all[lo:lo + HEAD_PAD], qgain_a) * q_scale
        qa_ref[0, lo:lo + HEAD_PAD, tok] = qh.astype(BF16)
        kvh = kv_all[lo:lo + HEAD_PAD]
        va_ref[0, hd * MLA_VS:hd * MLA_VS + MLA_V, tok] = kvh[MLA_NOPE:].astype(BF16)
        va_ref[0, hd * MLA_VS + MLA_V:(hd + 1) * MLA_VS, tok] = sum_rows
        kh = jnp.concatenate([kvh[:MLA_NOPE], kr, pad_rows], axis=0)
        kh = head_norm_rope_a(kh, kgain_a)
        ka_ref[0, tok, lo:lo + HEAD_PAD] = kh.T.astype(BF16)
    emit_gates(4)
    emit_gates(5)

    qgain_b = row_consts("qgain_b")
    kgain_b = row_consts("kgain_b")
    qb_scale = LOG2E / math.sqrt(DIFF_HD)
    hb = DIFF_ROT // 2

    def head_norm_rope_b(xh, gain):
        r = lax.rsqrt(jnp.mean(xh * xh, axis=0, keepdims=True) + NORM_EPS)
        xh = xh * r * gain
        r1, r2 = _rope_rows(xh[:hb], xh[hb:DIFF_ROT], cos_b, sin_b)
        return jnp.concatenate([r1, r2, xh[DIFF_ROT:]], axis=0)

    zvb = proj_t(o_vb, o_g)
    k_parts = []
    for blk in range(2 * DIFF_HEADS):
        lo = blk * DIFF_HD
        qh = head_norm_rope_b(zqb[lo:lo + DIFF_HD], qgain_b) * qb_scale
        qb_ref[0, lo:lo + DIFF_HD, tok] = qh.astype(BF16)
        k_parts.append(head_norm_rope_b(zkb[lo:lo + DIFF_HD], kgain_b))
    for hd in range(DIFF_HEADS):
        k12 = jnp.concatenate(k_parts[2 * hd:2 * hd + 2], axis=0)
        kb_ref[0, tok, hd * DIFF_V:(hd + 1) * DIFF_V] = k12.T.astype(BF16)
        vb_ref[0, hd * DIFF_VS:hd * DIFF_VS + DIFF_V, tok] = zvb[hd * DIFF_V:(hd + 1) * DIFF_V].astype(BF16)
        vb_ref[0, hd * DIFF_VS + DIFF_V:(hd + 1) * DIFF_VS, tok] = sum_rows
    emit_gates(6)
    emit_gates(7)


def _prep(n_t, pos3, w):
    bsz, d, s = n_t.shape
    tm = PREP_TOKEN_TILE
    n_in = w["win_t"].shape[0]
    wa, wb = MLA_HEADS * HEAD_PAD, DIFF_HEADS * DIFF_V
    n_gate = n_in - (MLA_Q_LORA + MLA_KV_LORA + MLA_ROPE + 3 * wb)

    def fm(rows):
        return (jax.ShapeDtypeStruct((bsz, rows, s), BF16),
                pl.BlockSpec((1, rows, tm), lambda b, i: (b, 0, i)))

    def tmaj(cols):
        return (jax.ShapeDtypeStruct((bsz, s, cols), BF16),
                pl.BlockSpec((1, tm, cols), lambda b, i: (b, i, 0)))

    outs = [fm(wa), tmaj(wa), fm(MLA_HEADS * MLA_VS), fm(wb), tmaj(wb), fm(DIFF_HEADS * DIFF_VS), fm(n_gate)]
    consts = [w["win_t"], w["wuq_t"], w["wukv_t"], w["table"]]
    in_specs = [
        pl.BlockSpec((1, d, tm), lambda b, i: (b, 0, i)),
        pl.BlockSpec((1, 1, tm), lambda b, i: (b, 0, i)),
    ] + [_const_spec(a.shape) for a in consts]
    return pl.pallas_call(
        _prep_kernel,
        grid=(bsz, s // tm),
        in_specs=in_specs,
        out_specs=[o[1] for o in outs],
        out_shape=[o[0] for o in outs],
        compiler_params=pltpu.CompilerParams(
            dimension_semantics=("arbitrary", "arbitrary"),
            vmem_limit_bytes=V7X_VMEM_LIMIT),
        name="prep",
    )(n_t, pos3, *consts)


def _normalise(acc, rows):
    return acc[:rows] * (1.0 / acc[rows:rows + 1])


def _attention_pipeline(n_heads, s_len, tq, kc, n_streams, scores_fn, v_fn, finish_fn):
    units = [(hd, q0, c0) for hd in range(n_heads) for q0 in range(0, s_len, tq)
             for c0 in range(0, s_len, kc)]
    streams = range(n_streams)
    n = len(units)
    s_cur = [scores_fn(*units[0], t) for t in streams]
    m = [None] * n_streams
    acc = [None] * n_streams
    pending = None
    for i in range(n + 1):
        s_nxt = [scores_fn(*units[i + 1], t) for t in streams] if i + 1 < n else None
        if pending is not None:
            (hd, q0, c0), alphas, probs = pending
            v = v_fn(hd, c0)
            for t in streams:
                o = jnp.dot(v, probs[t], preferred_element_type=F32)
                acc[t] = o if alphas[t] is None else alphas[t] * acc[t] + o
            if c0 + kc == s_len:
                finish_fn(hd, q0, acc)
                acc = [None] * n_streams
            pending = None
        if i < n:
            _, _, c0 = units[i]
            alphas, probs = [], []
            for t in streams:
                cmax = jnp.max(s_cur[t], axis=0, keepdims=True)
                if c0 == 0:
                    m[t] = cmax
                    alphas.append(None)
                else:
                    m_new = jnp.maximum(m[t], cmax)
                    alphas.append(jnp.exp2(m[t] - m_new))
                    m[t] = m_new
                probs.append(jnp.exp2(s_cur[t] - m[t]).astype(BF16))
            pending = (units[i], alphas, probs)
            s_cur = s_nxt


def _mla_kernel(k_ref, q_ref, v_ref, o_ref):
    def scores(hd, q0, c0, t):
        return jnp.dot(k_ref[0, c0:c0 + MLA_KEY_CHUNK, hd * HEAD_PAD:(hd + 1) * HEAD_PAD],
                       q_ref[0, hd * HEAD_PAD:(hd + 1) * HEAD_PAD, q0:q0 + MLA_Q_TILE],
                       preferred_element_type=F32)

    def values(hd, c0):
        return v_ref[0, hd * MLA_VS:(hd + 1) * MLA_VS, c0:c0 + MLA_KEY_CHUNK]

    def finish(hd, q0, accs):
        o_ref[0, hd * MLA_V:(hd + 1) * MLA_V, q0:q0 + MLA_Q_TILE] = (
            _normalise(accs[0], MLA_V).astype(BF16))

    _attention_pipeline(MLA_HEADS_PER_STEP, k_ref.shape[1], MLA_Q_TILE, MLA_KEY_CHUNK, 1,
                        scores, values, finish)


def _mla_attention(ka, qa, va):
    bsz, s, _ = ka.shape
    hp = MLA_HEADS_PER_STEP
    return pl.pallas_call(
        _mla_kernel,
        grid=(bsz, MLA_HEADS // hp),
        in_specs=[
            pl.BlockSpec((1, s, hp * HEAD_PAD), lambda b, h: (b, 0, h)),
            pl.BlockSpec((1, hp * HEAD_PAD, s), lambda b, h: (b, h, 0)),
            pl.BlockSpec((1, hp * MLA_VS, s), lambda b, h: (b, h, 0)),
        ],
        out_specs=pl.BlockSpec((1, hp * MLA_V, s), lambda b, h: (b, h, 0)),
        out_shape=jax.ShapeDtypeStruct((bsz, MLA_HEADS * MLA_V, s), BF16),
        compiler_params=pltpu.CompilerParams(
            dimension_semantics=("arbitrary", "arbitrary"),
            vmem_limit_bytes=V7X_VMEM_LIMIT),
        name="mla_attn",
    )(ka, qa, va)


def _diff_kernel(k_ref, q_ref, v_ref, lq1_ref, lk1_ref, lq2_ref, lk2_ref, subln_ref, o_ref, *,
                 lambda_init):
    lam = (jnp.exp(jnp.sum(lq1_ref[...] * lk1_ref[...], axis=-1, keepdims=True))
           - jnp.exp(jnp.sum(lq2_ref[...] * lk2_ref[...], axis=-1, keepdims=True))
           + lambda_init)
    subln = _lane_tile(subln_ref[...], DIFF_Q_TILE)
    zeros = jnp.zeros((DIFF_HD, DIFF_Q_TILE), BF16)

    def scores(hd, q0, c0, t):
        lo = hd * DIFF_V
        q12 = q_ref[0, lo:lo + DIFF_V, q0:q0 + DIFF_Q_TILE]
        k12 = k_ref[0, c0:c0 + DIFF_KEY_CHUNK, lo:lo + DIFF_V]
        if t == 0:
            q = jnp.concatenate([q12[:DIFF_HD], zeros], axis=0)
        else:
            q = jnp.concatenate([zeros, q12[DIFF_HD:]], axis=0)
        return jnp.dot(k12, q, preferred_element_type=F32)

    def values(hd, c0):
        return v_ref[0, hd * DIFF_VS:(hd + 1) * DIFF_VS, c0:c0 + DIFF_KEY_CHUNK]

    def finish(hd, q0, accs):
        o = _normalise(accs[0], DIFF_V) - lam * _normalise(accs[1], DIFF_V)
        r = lax.rsqrt(jnp.mean(o * o, axis=0, keepdims=True) + NORM_EPS)
        o_ref[0, hd * DIFF_V:(hd + 1) * DIFF_V, q0:q0 + DIFF_Q_TILE] = (
            (o * r * subln) * (1.0 - lambda_init)).astype(BF16)

    _attention_pipeline(DIFF_HEADS_PER_STEP, k_ref.shape[1], DIFF_Q_TILE, DIFF_KEY_CHUNK, 2,
                        scores, values, finish)


def _diff_attention(kb, qb, vb, lq1, lk1, lq2, lk2, subln, lambda_init):
    bsz, s, _ = kb.shape
    hp = DIFF_HEADS_PER_STEP
    vec = _const_spec((1, DIFF_HD))
    return pl.pallas_call(
        functools.partial(_diff_kernel, lambda_init=lambda_init),
        grid=(bsz, DIFF_HEADS // hp),
        in_specs=[
            pl.BlockSpec((1, s, hp * DIFF_V), lambda b, h: (b, 0, h)),
            pl.BlockSpec((1, hp * DIFF_V, s), lambda b, h: (b, h, 0)),
            pl.BlockSpec((1, hp * DIFF_VS, s), lambda b, h: (b, h, 0)),
            vec, vec, vec, vec,
            _const_spec((DIFF_V, LANES)),
        ],
        out_specs=pl.BlockSpec((1, hp * DIFF_V, s), lambda b, h: (b, h, 0)),
        out_shape=jax.ShapeDtypeStruct((bsz, DIFF_HEADS * DIFF_V, s), BF16),
        compiler_params=pltpu.CompilerParams(
            dimension_semantics=("arbitrary", "arbitrary"),
            vmem_limit_bytes=V7X_VMEM_LIMIT),
        name="diff_attn",
    )(kb, qb, vb, lq1, lk1, lq2, lk2, subln)


def _merge_kernel(oa_ref, ob_ref, g_ref, h_ref, mod_ref, woa_ref, wob_ref, wout_ref, o_ref):
    d = h_ref.shape[2]
    gate = mod_ref[0, 5:6, :]
    groups = [slice(t0, t0 + MERGE_SUB_TILE) for t0 in range(0, h_ref.shape[1], MERGE_SUB_TILE)]

    def branch_mix(tok):
        ya = jnp.dot(woa_ref[...], oa_ref[0, :, tok], preferred_element_type=F32)
        yb = jnp.dot(wob_ref[...], ob_ref[0, :, tok], preferred_element_type=F32)
        ga = g_ref[0, :d, tok].astype(F32)
        gb = g_ref[0, d:, tok].astype(F32)
        return (ga * ya + gb * yb).astype(BF16)

    def project(tok, mix):
        y_t = jnp.dot(wout_ref[...], mix, preferred_element_type=F32)
        o_ref[0, tok, :] = h_ref[0, tok, :] + gate * y_t.T

    mixes = [branch_mix(groups[0])]
    for prev, cur in zip(groups[:-1], groups[1:]):
        mixes.append(branch_mix(cur))
        project(prev, mixes[-2])
    project(groups[-1], mixes[-1])


def _merge(oa, ob, g, h, mod, woa_t, wob_t, wout_t):
    bsz, s, d = h.shape
    tm = MERGE_TOKEN_TILE
    return pl.pallas_call(
        _merge_kernel,
        grid=(bsz, s // tm),
        in_specs=[
            pl.BlockSpec((1, oa.shape[1], tm), lambda b, i: (b, 0, i)),
            pl.BlockSpec((1, ob.shape[1], tm), lambda b, i: (b, 0, i)),
            pl.BlockSpec((1, g.shape[1], tm), lambda b, i: (b, 0, i)),
            pl.BlockSpec((1, tm, d), lambda b, i: (b, i, 0)),
            pl.BlockSpec((1, N_MOD, d), lambda b, i: (b, 0, 0)),
            _const_spec(woa_t.shape),
            _const_spec(wob_t.shape),
            _const_spec(wout_t.shape),
        ],
        out_specs=pl.BlockSpec((1, tm, d), lambda b, i: (b, i, 0)),
        out_shape=jax.ShapeDtypeStruct((bsz, s, d), F32),
        compiler_params=pltpu.CompilerParams(
            dimension_semantics=("arbitrary", "arbitrary"),
            vmem_limit_bytes=V7X_VMEM_LIMIT),
        name="merge",
    )(oa, ob, g, h, mod, woa_t, wob_t, wout_t)


def _lane_bcast(v, rows=None):
    n = v.shape[0]
    out = jnp.broadcast_to(v.astype(F32)[:, None], (n, LANES))
    if rows is not None and rows > n:
        out = jnp.pad(out, ((0, rows - n), (0, 0)))
    return out


def _rope_freqs(half, theta):
    return 1.0 / (theta ** (jnp.arange(half, dtype=F32) / half))


def _prep_table(vectors):
    parts = []
    for name, (_, rows) in PREP_TABLE.items():
        v = vectors[name].astype(F32)
        parts.append(jnp.pad(v, (0, rows - v.shape[0])))
    return jnp.concatenate(parts)


def _layer_weights(l, w_in, mla_q_norm, mla_w_uq, mla_kv_norm, mla_w_ukv, mla_q_gain, mla_k_gain,
                   diff_q_gain, diff_k_gain):
    wuq = mla_w_uq[l].reshape(MLA_Q_LORA, MLA_HEADS, MLA_QK)
    wuq = jnp.pad(wuq, ((0, 0), (0, 0), (0, HEAD_PAD - MLA_QK))).reshape(MLA_Q_LORA, -1)
    return {
        "win_t": w_in[l].T.astype(BF16),
        "wuq_t": wuq.T.astype(BF16),
        "wukv_t": mla_w_ukv[l].T.astype(BF16),
        "table": _lane_bcast(_prep_table({
            "qnorm": mla_q_norm[l], "kvnorm": mla_kv_norm[l],
            "qgain_a": mla_q_gain[l], "kgain_a": mla_k_gain[l],
            "qgain_b": diff_q_gain[l], "kgain_b": diff_k_gain[l],
            "freq_a": _rope_freqs(MLA_ROPE // 2, MLA_THETA),
            "freq_b": _rope_freqs(DIFF_ROT // 2, DIFF_THETA)})),
    }


def kernel(x, c, positions, w_ada, b_ada, ffn1_norm, ffn1_w_gate, ffn1_w_up, ffn1_w_down, mix_norm, w_in, mla_q_norm, mla_w_uq, mla_kv_norm, mla_w_ukv, mla_q_gain, mla_k_gain, mla_w_o, diff_q_gain, diff_k_gain, diff_lambda_q1, diff_lambda_k1, diff_lambda_q2, diff_lambda_k2, diff_subln, diff_w_o, w_out, ffn2_norm, ffn2_w_gate, ffn2_w_up, ffn2_w_down, final_norm):
    bsz, s, d = x.shape
    depth = w_ada.shape[0]
    pos3 = positions.reshape(bsz, 1, s)
    h = x
    for l in range(depth):
        lambda_init = 0.8 - 0.6 * math.exp(-0.3 * l)
        mod = _ada(c, w_ada[l], b_ada[l]).reshape(bsz, N_MOD, d)

        h, n_mix = _ffn(h, mod, ffn1_norm[l], ffn1_w_gate[l], ffn1_w_up[l], ffn1_w_down[l],
                        sub=0, next_gain=mix_norm[l])

        w = _layer_weights(l, w_in, mla_q_norm, mla_w_uq, mla_kv_norm, mla_w_ukv, mla_q_gain,
                           mla_k_gain, diff_q_gain, diff_k_gain)
        qa, ka, va, qb, kb, vb, g = _prep(n_mix, pos3, w)
        oa = _mla_attention(ka, qa, va)
        ob = _diff_attention(kb, qb, vb,
                             diff_lambda_q1[l].reshape(1, -1), diff_lambda_k1[l].reshape(1, -1),
                             diff_lambda_q2[l].reshape(1, -1), diff_lambda_k2[l].reshape(1, -1),
                             _lane_bcast(diff_subln[l]), lambda_init)
        h = _merge(oa, ob, g, h, mod, mla_w_o[l].T.astype(BF16), diff_w_o[l].T.astype(BF16),
                   w_out[l].T.astype(BF16))

        h = _ffn(h, mod, ffn2_norm[l], ffn2_w_gate[l], ffn2_w_up[l], ffn2_w_down[l],
                 sub=2, final_gain=final_norm[l])
    return h
```
